```python
import jax
import jax.numpy as jnp
from jax import lax
import numpy as np

D_MODEL = 2048
BATCH = 4
SEQ = 8192
DEPTH = 2
DEC_BATCH = 8
DEC_SEQ = 64
PAST_LEN = 4096

CHUNK = 64
Q_BLOCK = 128
SB_HEAD_DIM = 128
SB_HEADS = D_MODEL // (2 * SB_HEAD_DIM)
SB_WIDTH = SB_HEADS * SB_HEAD_DIM
CONV_CH = D_MODEL - SB_WIDTH
CONV_WIDTH = 31
D_FF = 256 * ((8 * D_MODEL // 3 + 255) // 256)
IN_COLS = 3 * SB_WIDTH + 2 * CONV_CH
EPS = 1e-6

kernel_name = 'stickbreak_conformer_macaron_stream'


def rms_norm(x, g):
    xf = x.astype(jnp.float32)
    y = xf * lax.rsqrt(jnp.mean(xf * xf, axis=-1, keepdims=True) + EPS)
    return (y * g.astype(jnp.float32)).astype(x.dtype)


def layer_norm(x, g, b):
    xf = x.astype(jnp.float32)
    mu = jnp.mean(xf, axis=-1, keepdims=True)
    xc = xf - mu
    var = jnp.mean(xc * xc, axis=-1, keepdims=True)
    y = xc * lax.rsqrt(var + EPS) * g.astype(jnp.float32) + b.astype(jnp.float32)
    return y.astype(x.dtype)


def swiglu(h, w_gate, w_up, w_down):
    return (jax.nn.silu(h @ w_gate) * (h @ w_up)) @ w_down


def sb_block(q, k, v, q_pos, k_pos):
    z = jnp.einsum('bqhd,bkhd->bhqk', q, k, preferred_element_type=jnp.float32)
    z = z * (SB_HEAD_DIM ** -0.5)
    causal = k_pos[None, :] < q_pos[:, None]
    sp = jnp.where(causal, jax.nn.softplus(z), 0.0)
    between = lax.cumsum(sp, axis=3, reverse=True) - sp
    log_w = jax.nn.log_sigmoid(z) - between
    w = jnp.where(causal, jnp.exp(log_w), 0.0)
    return jnp.einsum('bhqk,bkhd->bqhd', w.astype(v.dtype), v)


def sb_prompt(q, k, v):
    B, S, H, d = q.shape
    nb = S // Q_BLOCK
    qb = q.reshape(B, nb, Q_BLOCK, H, d).transpose(1, 0, 2, 3, 4)
    pos = jnp.arange(S, dtype=jnp.int32)
    qpos = pos.reshape(nb, Q_BLOCK)
    out = lax.map(lambda a: sb_block(a[0], k, v, a[1], pos), (qb, qpos))
    return out.transpose(1, 0, 2, 3, 4).reshape(B, S, H, d)


def conv_module(u_c, hist, dw_w, dw_b, ln_g, ln_b):
    a = u_c[..., :CONV_CH] * jax.nn.sigmoid(u_c[..., CONV_CH:])
    full = jnp.concatenate([hist.astype(a.dtype), a], axis=1)
    y = lax.conv_general_dilated(
        full, dw_w[:, None, :].astype(full.dtype), window_strides=(1,), padding='VALID',
        dimension_numbers=('NWC', 'WIO', 'NWC'), feature_group_count=CONV_CH)
    y = y + dw_b
    y = jax.nn.silu(layer_norm(y, ln_g, ln_b))
    new_hist = full[:, -(CONV_WIDTH - 1):]
    return y, new_hist


def hybrid_layer(x, past_k, past_v, conv_hist,
                 g_ffn1, w1_gate, w1_up, w1_down, g_mix, w_in, dw_w, dw_b, ln_g, ln_b,
                 g_attn_out, g_conv_out, w_out, g_ffn2, w2_gate, w2_up, w2_down):
    B, T, _ = x.shape
    x = x + 0.5 * swiglu(rms_norm(x, g_ffn1), w1_gate, w1_up, w1_down)
    h = rms_norm(x, g_mix)
    u = h @ w_in
    q = u[..., :SB_WIDTH].reshape(B, T, SB_HEADS, SB_HEAD_DIM)
    k = u[..., SB_WIDTH:2 * SB_WIDTH].reshape(B, T, SB_HEADS, SB_HEAD_DIM)
    v = u[..., 2 * SB_WIDTH:3 * SB_WIDTH].reshape(B, T, SB_HEADS, SB_HEAD_DIM)
    u_c = u[..., 3 * SB_WIDTH:]
    if past_k is None:
        attn = sb_prompt(q, k, v)
        conv_hist = jnp.zeros((B, CONV_WIDTH - 1, CONV_CH), x.dtype)
    else:
        P = past_k.shape[1]
        k_all = jnp.concatenate([past_k.astype(k.dtype), k], axis=1)
        v_all = jnp.concatenate([past_v.astype(v.dtype), v], axis=1)
        k_pos = jnp.arange(P + T, dtype=jnp.int32)
        q_pos = P + jnp.arange(T, dtype=jnp.int32)
        attn = sb_block(q, k_all, v_all, q_pos, k_pos)
    conv_out, new_hist = conv_module(u_c, conv_hist, dw_w, dw_b, ln_g, ln_b)
    mixed = jnp.concatenate([rms_norm(attn.reshape(B, T, SB_WIDTH), g_attn_out),
                             rms_norm(conv_out, g_conv_out)], axis=-1)
    x = x + mixed @ w_out
    x = x + 0.5 * swiglu(rms_norm(x, g_ffn2), w2_gate, w2_up, w2_down)
    return x, k, v, new_hist


def setup_inputs(seed: int = 0) -> dict:
    key = jax.random.key(seed)
    ks = jax.random.split(key, 24)
    f32 = jnp.float32

    def nrm(k, shape, scale):
        return jax.random.normal(k, shape, f32) * scale

    def gain(k, shape):
        return 1.0 + 0.02 * jax.random.normal(k, shape, f32)

    kv_shape = (DEPTH, DEC_BATCH, PAST_LEN, SB_HEADS, SB_HEAD_DIM)
    return {
        'x_prompt': nrm(ks[0], (BATCH, SEQ, D_MODEL), 1.0),
        'x_sample': nrm(ks[1], (DEC_BATCH, DEC_SEQ, D_MODEL), 1.0),
        'cache_k': nrm(ks[2], kv_shape, 1.0),
        'cache_v': nrm(ks[3], kv_shape, 1.0),
        'state_conv': nrm(ks[4], (DEPTH, DEC_BATCH, CONV_WIDTH - 1, CONV_CH), 0.5),
        'norm_ffn1': gain(ks[5], (DEPTH, D_MODEL)),
        'ffn1_gate': nrm(ks[6], (DEPTH, D_MODEL, D_FF), D_MODEL ** -0.5),
        'ffn1_up': nrm(ks[7], (DEPTH, D_MODEL, D_FF), D_MODEL ** -0.5),
        'ffn1_down': nrm(ks[8], (DEPTH, D_FF, D_MODEL), D_FF ** -0.5),
        'norm_mix': gain(ks[9], (DEPTH, D_MODEL)),
        'w_in': nrm(ks[10], (DEPTH, D_MODEL, IN_COLS), D_MODEL ** -0.5),
        'dw_weight': nrm(ks[11], (DEPTH, CONV_WIDTH, CONV_CH), CONV_WIDTH ** -0.5),
        'dw_bias': nrm(ks[12], (DEPTH, CONV_CH), 0.02),
        'conv_ln_gain': gain(ks[13], (DEPTH, CONV_CH)),
        'conv_ln_bias': nrm(ks[14], (DEPTH, CONV_CH), 0.02),
        'norm_attn_out': gain(ks[15], (DEPTH, SB_WIDTH)),
        'norm_conv_out': gain(ks[16], (DEPTH, CONV_CH)),
        'w_out': nrm(ks[17], (DEPTH, D_MODEL, D_MODEL), D_MODEL ** -0.5),
        'norm_ffn2': gain(ks[18], (DEPTH, D_MODEL)),
        'ffn2_gate': nrm(ks[19], (DEPTH, D_MODEL, D_FF), D_MODEL ** -0.5),
        'ffn2_up': nrm(ks[20], (DEPTH, D_MODEL, D_FF), D_MODEL ** -0.5),
        'ffn2_down': nrm(ks[21], (DEPTH, D_FF, D_MODEL), D_FF ** -0.5),
        'norm_final': gain(ks[22], (D_MODEL,)),
    }


def reference(x_prompt, x_sample, cache_k, cache_v, state_conv,
              norm_ffn1, ffn1_gate, ffn1_up, ffn1_down, norm_mix, w_in,
              dw_weight, dw_bias, conv_ln_gain, conv_ln_bias,
              norm_attn_out, norm_conv_out, w_out,
              norm_ffn2, ffn2_gate, ffn2_up, ffn2_down, norm_final):
    xp, xs = x_prompt, x_sample
    kp_l, vp_l, cp_l, ks_l, vs_l, cs_l = [], [], [], [], [], []
    for l in range(DEPTH):
        lw = (norm_ffn1[l], ffn1_gate[l], ffn1_up[l], ffn1_down[l], norm_mix[l], w_in[l],
              dw_weight[l], dw_bias[l], conv_ln_gain[l], conv_ln_bias[l],
              norm_attn_out[l], norm_conv_out[l], w_out[l],
              norm_ffn2[l], ffn2_gate[l], ffn2_up[l], ffn2_down[l])
        xp, kp, vp, cp = hybrid_layer(xp, None, None, None, *lw)
        xs, kk, vv, cs = hybrid_layer(xs, cache_k[l], cache_v[l], state_conv[l], *lw)
        kp_l.append(kp)
        vp_l.append(vp)
        cp_l.append(cp)
        ks_l.append(kk)
        vs_l.append(vv)
        cs_l.append(cs)
    y_prompt = rms_norm(xp, norm_final)
    y_sample = rms_norm(xs, norm_final)
    return (y_prompt, y_sample,
            jnp.stack(kp_l), jnp.stack(vp_l), jnp.stack(cp_l),
            jnp.stack(ks_l), jnp.stack(vs_l), jnp.stack(cs_l))
```

```python
import functools

import jax
import jax.numpy as jnp
from jax import lax
from jax.experimental import pallas as pl
from jax.experimental.pallas import tpu as pltpu

F32 = jnp.float32
BF16 = jnp.bfloat16

EPS = 1e-6
SB_HEAD_DIM = 128
CONV_WIDTH = 31
HIST = CONV_WIDTH - 1
HIST_PAD = 32

VMEM_LIMIT = 56 * 1024 * 1024

SB_DEAD_SUM = 106.0


def _params(*sem):
    return pltpu.CompilerParams(dimension_semantics=sem, vmem_limit_bytes=VMEM_LIMIT)


def _rms(x, g):
    return x * lax.rsqrt(jnp.mean(x * x, axis=-1, keepdims=True) + EPS) * g


def _ffn_kernel(x_ref, g_ref, wg_ref, wu_ref, wd_ref, o_ref, h_ref):
    @pl.when(pl.program_id(1) == 0)
    def _():
        x = x_ref[...]
        h_ref[...] = _rms(x, g_ref[...]).astype(BF16)
        o_ref[...] = x

    h = h_ref[...]
    gate = jnp.dot(h, wg_ref[...], preferred_element_type=F32)
    up = jnp.dot(h, wu_ref[...], preferred_element_type=F32)
    act = (0.5 * gate * jax.nn.sigmoid(gate) * up).astype(BF16)
    o_ref[...] += jnp.dot(act, wd_ref[...], preferred_element_type=F32)


def _ffn(x, g, wg, wu, wd, *, tm=512, tf=512):
    m, d = x.shape
    f = wg.shape[1]
    tm = min(tm, m)
    assert m % tm == 0 and f % tf == 0
    return pl.pallas_call(
        _ffn_kernel,
        grid=(m // tm, f // tf),
        in_specs=[
            pl.BlockSpec((tm, d), lambda i, j: (i, 0)),
            pl.BlockSpec((1, d), lambda i, j: (0, 0)),
            pl.BlockSpec((d, tf), lambda i, j: (0, j)),
            pl.BlockSpec((d, tf), lambda i, j: (0, j)),
            pl.BlockSpec((tf, d), lambda i, j: (j, 0)),
        ],
        out_specs=pl.BlockSpec((tm, d), lambda i, j: (i, 0)),
        out_shape=jax.ShapeDtypeStruct((m, d), F32),
        scratch_shapes=[pltpu.VMEM((tm, d), BF16)],
        compiler_params=_params("parallel", "arbitrary"),
        name="ffn",
    )(x, g, wg, wu, wd)


def _in_proj_kernel(x_ref, g_ref, w_ref, q_ref, k_ref, v_ref, kb_ref, vb_ref, a_ref,
                    h_ref, val_ref):
    j = pl.program_id(1)

    @pl.when(j == 0)
    def _():
        h_ref[...] = _rms(x_ref[...], g_ref[...]).astype(BF16)

    u = jnp.dot(h_ref[...], w_ref[...], preferred_element_type=F32)

    @pl.when(j == 0)
    def _():
        q_ref[...] = u.astype(BF16)

    @pl.when(j == 1)
    def _():
        k_ref[...] = u
        kb_ref[...] = u.astype(BF16)

    @pl.when(j == 2)
    def _():
        v_ref[...] = u
        vb_ref[...] = u.astype(BF16)

    @pl.when(j == 3)
    def _():
        val_ref[...] = u

    @pl.when(j == 4)
    def _():
        a_ref[...] = val_ref[...] * jax.nn.sigmoid(u)


def _in_proj(x, g, w, *, tm=512):
    m, d = x.shape
    wd = w.shape[1] // 5
    tm = min(tm, m)
    assert m % tm == 0 and w.shape[1] == 5 * wd
    col = pl.BlockSpec((tm, wd), lambda i, j: (i, 0))
    return pl.pallas_call(
        _in_proj_kernel,
        grid=(m // tm, 5),
        in_specs=[
            pl.BlockSpec((tm, d), lambda i, j: (i, 0)),
            pl.BlockSpec((1, d), lambda i, j: (0, 0)),
            pl.BlockSpec((d, wd), lambda i, j: (0, j)),
        ],
        out_specs=[col] * 6,
        out_shape=[
            jax.ShapeDtypeStruct((m, wd), BF16),
            jax.ShapeDtypeStruct((m, wd), F32),
            jax.ShapeDtypeStruct((m, wd), F32),
            jax.ShapeDtypeStruct((m, wd), BF16),
            jax.ShapeDtypeStruct((m, wd), BF16),
            jax.ShapeDtypeStruct((m, wd), F32),
        ],
        scratch_shapes=[pltpu.VMEM((tm, d), BF16), pltpu.VMEM((tm, wd), F32)],
        compiler_params=_params("parallel", "arbitrary"),
        name="in_proj",
    )(x, g, w)


def _later_key_matrix(tk):
    j = lax.broadcasted_iota(jnp.int32, (tk, tk), 0)
    s = lax.broadcasted_iota(jnp.int32, (tk, tk), 1)
    return (j > s).astype(BF16)


def _sb_block(q, kb, vb, carry, mask):
    tk = kb.shape[0]
    z = lax.dot_general(q, kb, (((1,), (1,)), ((), ())), preferred_element_type=F32)
    z = z * (SB_HEAD_DIM ** -0.5)
    l = jnp.log(1.0 + jnp.exp(-jnp.abs(z)))
    sp = jnp.maximum(z, 0.0) + l
    log_beta = jnp.minimum(z, 0.0) - l
    if mask is not None:
        sp = jnp.where(mask, sp, 0.0)
    hi = sp.astype(BF16)
    lo = (sp - hi.astype(F32)).astype(BF16)
    u = _later_key_matrix(tk)
    between = (jnp.dot(hi, u, preferred_element_type=F32)
               + jnp.dot(lo, u, preferred_element_type=F32))
    w = jnp.exp(log_beta - between - carry)
    if mask is not None:
        w = jnp.where(mask, w, 0.0)
    pv = jnp.dot(w.astype(BF16), vb, preferred_element_type=F32)
    return pv, carry + jnp.sum(sp, axis=-1, keepdims=True)


def _sweep_earlier_blocks(q, load_kv, first_block, acc_ref, carry_ref):
    def cond(state):
        j, low = state
        return jnp.logical_and(j >= 0, low < SB_DEAD_SUM)

    def body(state):
        j, _ = state
        kb, vb = load_kv(j)
        pv, carry = _sb_block(q, kb, vb, carry_ref[...], None)
        acc_ref[...] += pv
        carry_ref[...] = carry
        return j - 1, jnp.min(carry)

    lax.while_loop(cond, body, (first_block, jnp.min(carry_ref[...])))


def _strictly_earlier(tq, tk):
    t = lax.broadcasted_iota(jnp.int32, (tq, tk), 0)
    s = lax.broadcasted_iota(jnp.int32, (tq, tk), 1)
    return s < t


def _attn_prompt_kernel(q_ref, k_ref, v_ref, o_ref, acc_ref, carry_ref, *, tq, heads):
    i = pl.program_id(2)
    mask = _strictly_earlier(tq, tq)
    for hh in range(heads):
        lanes = slice(hh * SB_HEAD_DIM, (hh + 1) * SB_HEAD_DIM)
        q = q_ref[0, :, lanes]

        def load_kv(j, lanes=lanes):
            rows = pl.ds(pl.multiple_of(j * tq, tq), tq)
            return k_ref[0, rows, lanes], v_ref[0, rows, lanes]

        kb, vb = load_kv(i)
        pv, carry = _sb_block(q, kb, vb, jnp.zeros((tq, 1), F32), mask)
        acc_ref[...] = pv
        carry_ref[...] = carry
        _sweep_earlier_blocks(q, load_kv, i - 1, acc_ref, carry_ref)
        o_ref[0, :, lanes] = acc_ref[...]


def _attn_prompt(q, k, v, *, tq=256, heads=2):
    b, s, w = q.shape
    hw = heads * SB_HEAD_DIM
    assert s % tq == 0 and w % hw == 0
    kv_spec = pl.BlockSpec((1, s, hw), lambda bi, g, i: (bi, 0, g))
    q_spec = pl.BlockSpec((1, tq, hw), lambda bi, g, i: (bi, i, g))
    return pl.pallas_call(
        functools.partial(_attn_prompt_kernel, tq=tq, heads=heads),
        grid=(b, w // hw, s // tq),
        in_specs=[q_spec, kv_spec, kv_spec],
        out_specs=q_spec,
        out_shape=jax.ShapeDtypeStruct((b, s, w), F32),
        scratch_shapes=[pltpu.VMEM((tq, SB_HEAD_DIM), F32), pltpu.VMEM((tq, 1), F32)],
        compiler_params=_params("parallel", "parallel", "arbitrary"),
        name="attn_prompt",
    )(q, k, v)


def _attn_sample_kernel(q_ref, kn_ref, vn_ref, kc_ref, vc_ref, o_ref, acc_ref, carry_ref,
                        *, tk, heads):
    t = q_ref.shape[1]
    past = kc_ref.shape[1]
    mask = _strictly_earlier(t, t)
    for hh in range(heads):
        lanes = slice(hh * SB_HEAD_DIM, (hh + 1) * SB_HEAD_DIM)
        q = q_ref[0, :, lanes]

        def load_kv(j, lanes=lanes):
            rows = pl.ds(pl.multiple_of(j * tk, tk), tk)
            return kc_ref[0, rows, lanes].astype(BF16), vc_ref[0, rows, lanes].astype(BF16)

        pv, carry = _sb_block(q, kn_ref[0, :, lanes], vn_ref[0, :, lanes],
                              jnp.zeros((t, 1), F32), mask)
        acc_ref[...] = pv
        carry_ref[...] = carry
        _sweep_earlier_blocks(q, load_kv, past // tk - 1, acc_ref, carry_ref)
        o_ref[0, :, lanes] = acc_ref[...]


def _attn_sample(q, k_new, v_new, k_past, v_past, *, tk=128, heads=2):
    b, t, w = q.shape
    p = k_past.shape[1]
    hw = heads * SB_HEAD_DIM
    assert p % tk == 0 and w % hw == 0
    new_spec = pl.BlockSpec((1, t, hw), lambda bi, g: (bi, 0, g))
    past_spec = pl.BlockSpec((1, p, hw), lambda bi, g: (bi, 0, g))
    return pl.pallas_call(
        functools.partial(_attn_sample_kernel, tk=tk, heads=heads),
        grid=(b, w // hw),
        in_specs=[new_spec, new_spec, new_spec, past_spec, past_spec],
        out_specs=new_spec,
        out_shape=jax.ShapeDtypeStruct((b, t, w), F32),
        scratch_shapes=[pltpu.VMEM((t, SB_HEAD_DIM), F32), pltpu.VMEM((t, 1), F32)],
        compiler_params=_params("parallel", "parallel"),
        name="attn_sample",
    )(q, k_new, v_new, k_past, v_past)


def _conv_kernel(a_ref, prev_ref, hist_ref, dw_ref, db_ref, lg_ref, lb_ref, go_ref, o_ref,
                 full_ref, *, tt, rows):
    i = pl.program_id(1)

    @pl.when(i == 0)
    def _():
        full_ref[HIST_PAD - HIST:HIST_PAD, :] = hist_ref[0]

    @pl.when(i > 0)
    def _():
        full_ref[0:HIST_PAD, :] = prev_ref[0]

    full_ref[HIST_PAD:HIST_PAD + tt, :] = a_ref[0]

    for r0 in range(0, tt, rows):
        base = r0 + HIST_PAD - HIST
        y = full_ref[base:base + rows, :] * dw_ref[0:1, :]
        for wi in range(1, CONV_WIDTH):
            y = y + full_ref[base + wi:base + wi + rows, :] * dw_ref[wi:wi + 1, :]
        y = y + db_ref[...]
        mu = jnp.mean(y, axis=-1, keepdims=True)
        yc = y - mu
        var = jnp.mean(yc * yc, axis=-1, keepdims=True)
        y = yc * lax.rsqrt(var + EPS) * lg_ref[...] + lb_ref[...]
        y = y * jax.nn.sigmoid(y)
        o_ref[0, r0:r0 + rows, :] = _rms(y, go_ref[...]).astype(o_ref.dtype)


def _conv(a, hist, dw_w, dw_b, ln_g, ln_b, g_out, *, tt=256, rows=32):
    b, t, c = a.shape
    tt = min(tt, t)
    assert t % tt == 0 and tt % rows == 0 and tt % HIST_PAD == 0
    per = tt // HIST_PAD
    vec = pl.BlockSpec((1, c), lambda bi, i: (0, 0))
    return pl.pallas_call(
        functools.partial(_conv_kernel, tt=tt, rows=rows),
        grid=(b, t // tt),
        in_specs=[
            pl.BlockSpec((1, tt, c), lambda bi, i: (bi, i, 0)),
            pl.BlockSpec((1, HIST_PAD, c), lambda bi, i: (bi, jnp.maximum(i * per - 1, 0), 0)),
            pl.BlockSpec((1, HIST, c), lambda bi, i: (bi, 0, 0)),
            pl.BlockSpec((CONV_WIDTH, c), lambda bi, i: (0, 0)),
            vec, vec, vec, vec,
        ],
        out_specs=pl.BlockSpec((1, tt, c), lambda bi, i: (bi, i, 0)),
        out_shape=jax.ShapeDtypeStruct((b, t, c), BF16),
        scratch_shapes=[pltpu.VMEM((HIST_PAD + tt, c), F32)],
        compiler_params=_params("parallel", "arbitrary"),
        name="conv",
    )(a, a, hist, dw_w, dw_b, ln_g, ln_b, g_out)


def _out_proj_kernel(x_ref, attn_ref, conv_ref, ga_ref, w_ref, o_ref, m_ref):
    sb = attn_ref.shape[1]

    @pl.when(pl.program_id(1) == 0)
    def _():
        m_ref[:, :sb] = _rms(attn_ref[...], ga_ref[...]).astype(BF16)
        m_ref[:, sb:] = conv_ref[...]

    o_ref[...] = x_ref[...] + jnp.dot(m_ref[...], w_ref[...], preferred_element_type=F32)


def _out_proj(x, attn, conv_n, g_attn, w, *, tm=512, tn=1024):
    m, d = x.shape
    sb, c = attn.shape[1], conv_n.shape[1]
    tm, tn = min(tm, m), min(tn, d)
    assert m % tm == 0 and d % tn == 0 and w.shape == (sb + c, d)
    return pl.pallas_call(
        _out_proj_kernel,
        grid=(m // tm, d // tn),
        in_specs=[
            pl.BlockSpec((tm, tn), lambda i, j: (i, j)),
            pl.BlockSpec((tm, sb), lambda i, j: (i, 0)),
            pl.BlockSpec((tm, c), lambda i, j: (i, 0)),
            pl.BlockSpec((1, sb), lambda i, j: (0, 0)),
            pl.BlockSpec((sb + c, tn), lambda i, j: (0, j)),
        ],
        out_specs=pl.BlockSpec((tm, tn), lambda i, j: (i, j)),
        out_shape=jax.ShapeDtypeStruct((m, d), F32),
        scratch_shapes=[pltpu.VMEM((tm, sb + c), BF16)],
        compiler_params=_params("parallel", "arbitrary"),
        name="out_proj",
    )(x, attn, conv_n, g_attn, w)


def _final_norm_kernel(x_ref, g_ref, o_ref):
    o_ref[...] = _rms(x_ref[...], g_ref[...])


def _final_norm(x, g, *, tm=512):
    m, d = x.shape
    tm = min(tm, m)
    assert m % tm == 0
    return pl.pallas_call(
        _final_norm_kernel,
        grid=(m // tm,),
        in_specs=[pl.BlockSpec((tm, d), lambda i: (i, 0)), pl.BlockSpec((1, d), lambda i: (0, 0))],
        out_specs=pl.BlockSpec((tm, d), lambda i: (i, 0)),
        out_shape=jax.ShapeDtypeStruct((m, d), F32),
        compiler_params=_params("parallel"),
        name="final_norm",
    )(x, g)


def _row(v):
    return v.reshape(1, -1).astype(F32)


def _layer(x, past_k, past_v, conv_hist, lw):
    b, t, d = x.shape
    x2 = x.reshape(b * t, d)
    x2 = _ffn(x2, lw["g_ffn1"], lw["w1_gate"], lw["w1_up"], lw["w1_down"])
    q, k, v, kb, vb, a = _in_proj(x2, lw["g_mix"], lw["w_in"])
    sbw = q.shape[1]
    heads = sbw // SB_HEAD_DIM
    seq = lambda y: y.reshape(b, t, y.shape[1])
    if past_k is None:
        attn = _attn_prompt(seq(q), seq(kb), seq(vb))
    else:
        p = past_k.shape[1]
        attn = _attn_sample(seq(q), seq(kb), seq(vb),
                            past_k.reshape(b, p, sbw), past_v.reshape(b, p, sbw))
    a3 = seq(a)
    conv_n = _conv(a3, conv_hist, lw["dw_w"], lw["dw_b"], lw["ln_g"], lw["ln_b"], lw["g_conv_out"])
    x2 = _out_proj(x2, attn.reshape(b * t, sbw), conv_n.reshape(b * t, -1), lw["g_attn_out"], lw["w_out"])
    x2 = _ffn(x2, lw["g_ffn2"], lw["w2_gate"], lw["w2_up"], lw["w2_down"])
    new_hist = a3[:, t - HIST:, :]
    return (x2.reshape(b, t, d), k.reshape(b, t, heads, SB_HEAD_DIM),
            v.reshape(b, t, heads, SB_HEAD_DIM), new_hist)


def kernel(x_prompt, x_sample, cache_k, cache_v, state_conv, norm_ffn1, ffn1_gate, ffn1_up, ffn1_down, norm_mix, w_in, dw_weight, dw_bias, conv_ln_gain, conv_ln_bias, norm_attn_out, norm_conv_out, w_out, norm_ffn2, ffn2_gate, ffn2_up, ffn2_down, norm_final):
    depth = w_in.shape[0]
    conv_ch = dw_weight.shape[2]
    assert x_prompt.shape[1] >= HIST and x_sample.shape[1] >= HIST
    xp, xs = x_prompt, x_sample
    zero_hist = jnp.zeros((x_prompt.shape[0], HIST, conv_ch), F32)
    outs = [[] for _ in range(6)]
    for l in range(depth):
        lw = dict(
            g_ffn1=_row(norm_ffn1[l]), w1_gate=ffn1_gate[l].astype(BF16), w1_up=ffn1_up[l].astype(BF16),
            w1_down=ffn1_down[l].astype(BF16), g_mix=_row(norm_mix[l]), w_in=w_in[l].astype(BF16),
            dw_w=dw_weight[l].astype(F32), dw_b=_row(dw_bias[l]), ln_g=_row(conv_ln_gain[l]),
            ln_b=_row(conv_ln_bias[l]), g_attn_out=_row(norm_attn_out[l]), g_conv_out=_row(norm_conv_out[l]),
            w_out=w_out[l].astype(BF16), g_ffn2=_row(norm_ffn2[l]), w2_gate=ffn2_gate[l].astype(BF16),
            w2_up=ffn2_up[l].astype(BF16), w2_down=ffn2_down[l].astype(BF16))
        xp, kp, vp, cp = _layer(xp, None, None, zero_hist, lw)
        xs, kk, vv, cs = _layer(xs, cache_k[l], cache_v[l], state_conv[l], lw)
        for lst, val in zip(outs, (kp, vp, cp, kk, vv, cs)):
            lst.append(val)
    g_final = _row(norm_final)
    y_prompt = _final_norm(xp.reshape(-1, xp.shape[-1]), g_final).reshape(xp.shape)
    y_sample = _final_norm(xs.reshape(-1, xs.shape[-1]), g_final).reshape(xs.shape)
    return (y_prompt, y_sample) + tuple(jnp.stack(o) for o in outs)
```

```python
import functools

import jax
import jax.numpy as jnp
from jax import lax
from jax.experimental import pallas as pl
from jax.experimental.pallas import tpu as pltpu

F32 = jnp.float32
BF16 = jnp.bfloat16

EPS = 1e-6
SB_HEAD_DIM = 128
CONV_WIDTH = 31
HIST = CONV_WIDTH - 1
SUBLANES = 8
HIST_PAD = 32

VMEM_LIMIT = 56 * 1024 * 1024

SB_DEAD_SUM = 106.0


def _params(*sem):
    return pltpu.CompilerParams(dimension_semantics=sem, vmem_limit_bytes=VMEM_LIMIT)


def _rms(x, g):
    return x * lax.rsqrt(jnp.mean(x * x, axis=-1, keepdims=True) + EPS) * g


def _ffn_kernel(x_ref, g_ref, wg_ref, wu_ref, wd_ref, o_ref, h_ref):
    @pl.when(pl.program_id(1) == 0)
    def _():
        x = x_ref[...]
        h_ref[...] = _rms(x, g_ref[...]).astype(BF16)
        o_ref[...] = x

    h = h_ref[...]
    gate = jnp.dot(h, wg_ref[...], preferred_element_type=F32)
    up = jnp.dot(h, wu_ref[...], preferred_element_type=F32)
    act = (0.5 * gate * jax.nn.sigmoid(gate) * up).astype(BF16)
    o_ref[...] += jnp.dot(act, wd_ref[...], preferred_element_type=F32)


def _ffn(x, g, wg, wu, wd, *, tm=1024, tf=512):
    m, d = x.shape
    f = wg.shape[1]
    tm = min(tm, m)
    assert m % tm == 0 and f % tf == 0
    return pl.pallas_call(
        _ffn_kernel,
        grid=(m // tm, f // tf),
        in_specs=[
            pl.BlockSpec((tm, d), lambda i, j: (i, 0)),
            pl.BlockSpec((1, d), lambda i, j: (0, 0)),
            pl.BlockSpec((d, tf), lambda i, j: (0, j)),
            pl.BlockSpec((d, tf), lambda i, j: (0, j)),
            pl.BlockSpec((tf, d), lambda i, j: (j, 0)),
        ],
        out_specs=pl.BlockSpec((tm, d), lambda i, j: (i, 0)),
        out_shape=jax.ShapeDtypeStruct((m, d), F32),
        scratch_shapes=[pltpu.VMEM((tm, d), BF16)],
        compiler_params=_params("parallel", "arbitrary"),
        name="ffn",
    )(x, g, wg, wu, wd)


def _in_proj_kernel(x_ref, g_ref, w_ref, q_ref, k_ref, v_ref, kb_ref, vb_ref, a_ref):
    wd = q_ref.shape[1]
    h = _rms(x_ref[...], g_ref[...]).astype(BF16)

    def group(j):
        return jnp.dot(h, w_ref[:, j * wd:(j + 1) * wd], preferred_element_type=F32)

    q_ref[...] = group(0).astype(BF16)
    k = group(1)
    k_ref[...] = k
    kb_ref[...] = k.astype(BF16)
    v = group(2)
    v_ref[...] = v
    vb_ref[...] = v.astype(BF16)
    a_ref[...] = group(3) * jax.nn.sigmoid(group(4))


def _in_proj(x, g, w, *, tm=512):
    m, d = x.shape
    wd = w.shape[1] // 5
    tm = min(tm, m)
    assert m % tm == 0 and w.shape[1] == 5 * wd
    col = pl.BlockSpec((tm, wd), lambda i: (i, 0))
    return pl.pallas_call(
        _in_proj_kernel,
        grid=(m // tm,),
        in_specs=[
            pl.BlockSpec((tm, d), lambda i: (i, 0)),
            pl.BlockSpec((1, d), lambda i: (0, 0)),
            pl.BlockSpec((d, 5 * wd), lambda i: (0, 0)),
        ],
        out_specs=[col] * 6,
        out_shape=[
            jax.ShapeDtypeStruct((m, wd), BF16),
            jax.ShapeDtypeStruct((m, wd), F32),
            jax.ShapeDtypeStruct((m, wd), F32),
            jax.ShapeDtypeStruct((m, wd), BF16),
            jax.ShapeDtypeStruct((m, wd), BF16),
            jax.ShapeDtypeStruct((m, wd), F32),
        ],
        compiler_params=_params("parallel"),
        name="in_proj",
    )(x, g, w)


def _later_key_matrix(tk):
    j = lax.broadcasted_iota(jnp.int32, (tk, tk), 0)
    s = lax.broadcasted_iota(jnp.int32, (tk, tk), 1)
    return (j > s).astype(BF16)


def _sb_block(q, kb, vb, carry, mask, later):
    z = lax.dot_general(q, kb, (((1,), (1,)), ((), ())), preferred_element_type=F32)
    z = z * (SB_HEAD_DIM ** -0.5)
    l = jnp.log(1.0 + jnp.exp(-jnp.abs(z)))
    sp = jnp.maximum(z, 0.0) + l
    log_beta = jnp.minimum(z, 0.0) - l
    if mask is not None:
        sp = jnp.where(mask, sp, 0.0)
    hi = sp.astype(BF16)
    lo = (sp - hi.astype(F32)).astype(BF16)
    between = (jnp.dot(hi, later, preferred_element_type=F32)
               + jnp.dot(lo, later, preferred_element_type=F32))
    w = jnp.exp(log_beta - between - carry)
    if mask is not None:
        w = jnp.where(mask, w, 0.0)
    pv = jnp.dot(w.astype(BF16), vb, preferred_element_type=F32)
    return pv, carry + jnp.sum(sp, axis=-1, keepdims=True)


def _attn_prompt_kernel(later_ref, q_ref, k_ref, v_ref, o_ref, carry_ref, *, tq, nprev, heads):
    i = pl.program_id(2)
    first = jnp.maximum(i - nprev, 0)
    t = lax.broadcasted_iota(jnp.int32, (tq, tq), 0) + (i - first) * tq
    s = lax.broadcasted_iota(jnp.int32, (tq, tq), 1)
    head_lanes = [slice(hh * SB_HEAD_DIM, (hh + 1) * SB_HEAD_DIM) for hh in range(heads)]

    low = None
    for hh, lanes in enumerate(head_lanes):
        q = q_ref[0, :, lanes]
        carry = jnp.zeros((tq, 1), F32)
        out = None
        for blk in range(nprev, -1, -1):
            rows = pl.ds(pl.multiple_of((first + blk) * tq, tq), tq)
            pv, carry = _sb_block(q, k_ref[0, rows, lanes], v_ref[0, rows, lanes], carry,
                                  s + blk * tq < t, later_ref[...])
            out = pv if out is None else out + pv
        o_ref[0, :, lanes] = out
        carry_ref[hh] = carry
        low = jnp.min(carry) if low is None else jnp.minimum(low, jnp.min(carry))

    def cond(state):
        j, low = state
        return jnp.logical_and(j >= 0, low < SB_DEAD_SUM)

    def body(state):
        j, _ = state
        rows = pl.ds(pl.multiple_of(j * tq, tq), tq)
        low = None
        for hh, lanes in enumerate(head_lanes):
            pv, carry = _sb_block(q_ref[0, :, lanes], k_ref[0, rows, lanes], v_ref[0, rows, lanes],
                                  carry_ref[hh], None, later_ref[...])
            o_ref[0, :, lanes] += pv
            carry_ref[hh] = carry
            low = jnp.min(carry) if low is None else jnp.minimum(low, jnp.min(carry))
        return j - 1, low

    lax.while_loop(cond, body, (first - 1, low))


def _attn_prompt(q, k, v, *, tq=256, nprev=1, heads=4):
    b, s, w = q.shape
    heads = min(heads, w // SB_HEAD_DIM)
    hw = heads * SB_HEAD_DIM
    assert s % tq == 0 and s >= (nprev + 1) * tq and w % hw == 0
    kv_spec = pl.BlockSpec((1, s, hw), lambda bi, g, i: (bi, 0, g))
    q_spec = pl.BlockSpec((1, tq, hw), lambda bi, g, i: (bi, i, g))
    return pl.pallas_call(
        functools.partial(_attn_prompt_kernel, tq=tq, nprev=nprev, heads=heads),
        grid=(b, w // hw, s // tq),
        in_specs=[pl.BlockSpec((tq, tq), lambda bi, g, i: (0, 0)), q_spec, kv_spec, kv_spec],
        out_specs=q_spec,
        out_shape=jax.ShapeDtypeStruct((b, s, w), F32),
        scratch_shapes=[pltpu.VMEM((heads, tq, 1), F32)],
        compiler_params=_params("parallel", "parallel", "arbitrary"),
        name="attn_prompt",
    )(_later_key_matrix(tq), q, k, v)


def _strictly_earlier(tq, tk):
    t = lax.broadcasted_iota(jnp.int32, (tq, tk), 0)
    s = lax.broadcasted_iota(jnp.int32, (tq, tk), 1)
    return s < t


def _attn_sample_kernel(q_ref, kn_ref, vn_ref, kc_ref, vc_ref, o_ref, carry_ref, *, tk, heads):
    t = q_ref.shape[1]
    past = kc_ref.shape[1]
    mask = _strictly_earlier(t, t)
    for hh in range(heads):
        lanes = slice(hh * SB_HEAD_DIM, (hh + 1) * SB_HEAD_DIM)
        q = q_ref[0, :, lanes]
        pv, carry = _sb_block(q, kn_ref[0, :, lanes], vn_ref[0, :, lanes],
                              jnp.zeros((t, 1), F32), mask, _later_key_matrix(t))
        o_ref[0, :, lanes] = pv
        carry_ref[...] = carry

        def cond(state):
            j, low = state
            return jnp.logical_and(j >= 0, low < SB_DEAD_SUM)

        def body(state, q=q, lanes=lanes):
            j, _ = state
            rows = pl.ds(pl.multiple_of(j * tk, tk), tk)
            pv, carry = _sb_block(q, kc_ref[0, rows, lanes].astype(BF16), vc_ref[0, rows, lanes].astype(BF16),
                                  carry_ref[...], None, _later_key_matrix(tk))
            o_ref[0, :, lanes] += pv
            carry_ref[...] = carry
            return j - 1, jnp.min(carry)

        lax.while_loop(cond, body, (past // tk - 1, jnp.min(carry)))


def _attn_sample(q, k_new, v_new, k_past, v_past, *, tk=128, heads=2):
    b, t, w = q.shape
    p = k_past.shape[1]
    hw = heads * SB_HEAD_DIM
    assert p % tk == 0 and w % hw == 0
    new_spec = pl.BlockSpec((1, t, hw), lambda bi, g: (bi, 0, g))
    past_spec = pl.BlockSpec((1, p, hw), lambda bi, g: (bi, 0, g))
    return pl.pallas_call(
        functools.partial(_attn_sample_kernel, tk=tk, heads=heads),
        grid=(b, w // hw),
        in_specs=[new_spec, new_spec, new_spec, past_spec, past_spec],
        out_specs=new_spec,
        out_shape=jax.ShapeDtypeStruct((b, t, w), F32),
        scratch_shapes=[pltpu.VMEM((t, 1), F32)],
        compiler_params=_params("parallel", "parallel"),
        name="attn_sample",
    )(q, k_new, v_new, k_past, v_past)


def _conv_kernel(a_ref, prev_ref, hist_ref, dw_ref, db_ref, lg_ref, lb_ref, go_ref, o_ref,
                 sh_ref, *, tt, rows):
    i = pl.program_id(1)
    c = a_ref.shape[2]

    @pl.when(i == 0)
    def _():
        sh_ref[0, 0:SUBLANES, :] = jnp.zeros((SUBLANES, c), F32)
        sh_ref[0, HIST_PAD - HIST:HIST_PAD, :] = hist_ref[0]

    @pl.when(i > 0)
    def _():
        sh_ref[0, 0:HIST_PAD, :] = prev_ref[0]

    sh_ref[0, HIST_PAD:HIST_PAD + tt, :] = a_ref[0]

    n_shift = tt + HIST_PAD - SUBLANES
    for b in range(1, SUBLANES):
        sh_ref[b, 0:n_shift, :] = sh_ref[0, b:b + n_shift, :]

    w0 = HIST_PAD - HIST
    for r0 in range(0, tt, rows):
        y = None
        for w in range(w0, w0 + CONV_WIDTH):
            base = r0 + (w // SUBLANES) * SUBLANES
            term = sh_ref[w % SUBLANES, base:base + rows, :] * dw_ref[w - w0:w - w0 + 1, :]
            y = term if y is None else y + term
        y = y + db_ref[...]
        mu = jnp.mean(y, axis=-1, keepdims=True)
        yc = y - mu
        var = jnp.mean(yc * yc, axis=-1, keepdims=True)
        y = yc * lax.rsqrt(var + EPS) * lg_ref[...] + lb_ref[...]
        y = y * jax.nn.sigmoid(y)
        o_ref[0, r0:r0 + rows, :] = _rms(y, go_ref[...]).astype(o_ref.dtype)


def _conv(a, hist, dw_w, dw_b, ln_g, ln_b, g_out, *, tt=256, rows=32):
    b, t, c = a.shape
    tt = min(tt, t)
    assert t % tt == 0 and tt % rows == 0 and tt % HIST_PAD == 0
    per = tt // HIST_PAD
    vec = pl.BlockSpec((1, c), lambda bi, i: (0, 0))
    return pl.pallas_call(
        functools.partial(_conv_kernel, tt=tt, rows=rows),
        grid=(b, t // tt),
        in_specs=[
            pl.BlockSpec((1, tt, c), lambda bi, i: (bi, i, 0)),
            pl.BlockSpec((1, HIST_PAD, c), lambda bi, i: (bi, jnp.maximum(i * per - 1, 0), 0)),
            pl.BlockSpec((1, HIST, c), lambda bi, i: (bi, 0, 0)),
            pl.BlockSpec((CONV_WIDTH, c), lambda bi, i: (0, 0)),
            vec, vec, vec, vec,
        ],
        out_specs=pl.BlockSpec((1, tt, c), lambda bi, i: (bi, i, 0)),
        out_shape=jax.ShapeDtypeStruct((b, t, c), BF16),
        scratch_shapes=[pltpu.VMEM((SUBLANES, HIST_PAD + tt, c), F32)],
        compiler_params=_params("parallel", "arbitrary"),
        name="conv",
    )(a, a, hist, dw_w, dw_b, ln_g, ln_b, g_out)


def _out_proj_kernel(x_ref, attn_ref, conv_ref, ga_ref, w_ref, o_ref):
    sb = attn_ref.shape[1]
    attn_n = _rms(attn_ref[...], ga_ref[...]).astype(BF16)
    o_ref[...] = (x_ref[...]
                  + jnp.dot(attn_n, w_ref[:sb, :], preferred_element_type=F32)
                  + jnp.dot(conv_ref[...], w_ref[sb:, :], preferred_element_type=F32))


def _out_proj(x, attn, conv_n, g_attn, w, *, tm=512):
    m, d = x.shape
    sb, c = attn.shape[1], conv_n.shape[1]
    tm = min(tm, m)
    assert m % tm == 0 and w.shape == (sb + c, d)
    return pl.pallas_call(
        _out_proj_kernel,
        grid=(m // tm,),
        in_specs=[
            pl.BlockSpec((tm, d), lambda i: (i, 0)),
            pl.BlockSpec((tm, sb), lambda i: (i, 0)),
            pl.BlockSpec((tm, c), lambda i: (i, 0)),
            pl.BlockSpec((1, sb), lambda i: (0, 0)),
            pl.BlockSpec((sb + c, d), lambda i: (0, 0)),
        ],
        out_specs=pl.BlockSpec((tm, d), lambda i: (i, 0)),
        out_shape=jax.ShapeDtypeStruct((m, d), F32),
        compiler_params=_params("parallel"),
        name="out_proj",
    )(x, attn, conv_n, g_attn, w)


def _final_norm_kernel(x_ref, g_ref, o_ref):
    o_ref[...] = _rms(x_ref[...], g_ref[...])


def _final_norm(x, g, *, tm=512):
    m, d = x.shape
    tm = min(tm, m)
    assert m % tm == 0
    return pl.pallas_call(
        _final_norm_kernel,
        grid=(m // tm,),
        in_specs=[pl.BlockSpec((tm, d), lambda i: (i, 0)), pl.BlockSpec((1, d), lambda i: (0, 0))],
        out_specs=pl.BlockSpec((tm, d), lambda i: (i, 0)),
        out_shape=jax.ShapeDtypeStruct((m, d), F32),
        compiler_params=_params("parallel"),
        name="final_norm",
    )(x, g)


def _row(v):
    return v.reshape(1, -1).astype(F32)


def _layer(x, past_k, past_v, conv_hist, lw):
    b, t, d = x.shape
    x2 = x.reshape(b * t, d)
    x2 = _ffn(x2, lw["g_ffn1"], lw["w1_gate"], lw["w1_up"], lw["w1_down"])
    q, k, v, kb, vb, a = _in_proj(x2, lw["g_mix"], lw["w_in"])
    sbw = q.shape[1]
    heads = sbw // SB_HEAD_DIM
    seq = lambda y: y.reshape(b, t, y.shape[1])
    if past_k is None:
        attn = _attn_prompt(seq(q), seq(kb), seq(vb))
    else:
        p = past_k.shape[1]
        attn = _attn_sample(seq(q), seq(kb), seq(vb),
                            past_k.reshape(b, p, sbw), past_v.reshape(b, p, sbw))
    a3 = seq(a)
    conv_n = _conv(a3, conv_hist, lw["dw_w"], lw["dw_b"], lw["ln_g"], lw["ln_b"], lw["g_conv_out"])
    x2 = _out_proj(x2, attn.reshape(b * t, sbw), conv_n.reshape(b * t, -1), lw["g_attn_out"], lw["w_out"])
    x2 = _ffn(x2, lw["g_ffn2"], lw["w2_gate"], lw["w2_up"], lw["w2_down"])
    new_hist = a3[:, t - HIST:, :]
    return (x2.reshape(b, t, d), k.reshape(b, t, heads, SB_HEAD_DIM),
            v.reshape(b, t, heads, SB_HEAD_DIM), new_hist)


def kernel(x_prompt, x_sample, cache_k, cache_v, state_conv, norm_ffn1, ffn1_gate, ffn1_up, ffn1_down, norm_mix, w_in, dw_weight, dw_bias, conv_ln_gain, conv_ln_bias, norm_attn_out, norm_conv_out, w_out, norm_ffn2, ffn2_gate, ffn2_up, ffn2_down, norm_final):
    depth = w_in.shape[0]
    conv_ch = dw_weight.shape[2]
    assert x_prompt.shape[1] >= HIST and x_sample.shape[1] >= HIST
    xp, xs = x_prompt, x_sample
    zero_hist = jnp.zeros((x_prompt.shape[0], HIST, conv_ch), F32)
    outs = [[] for _ in range(6)]
    for l in range(depth):
        lw = dict(
            g_ffn1=_row(norm_ffn1[l]), w1_gate=ffn1_gate[l].astype(BF16), w1_up=ffn1_up[l].astype(BF16),
            w1_down=ffn1_down[l].astype(BF16), g_mix=_row(norm_mix[l]), w_in=w_in[l].astype(BF16),
            dw_w=dw_weight[l].astype(F32), dw_b=_row(dw_bias[l]), ln_g=_row(conv_ln_gain[l]),
            ln_b=_row(conv_ln_bias[l]), g_attn_out=_row(norm_attn_out[l]), g_conv_out=_row(norm_conv_out[l]),
            w_out=w_out[l].astype(BF16), g_ffn2=_row(norm_ffn2[l]), w2_gate=ffn2_gate[l].astype(BF16),
            w2_up=ffn2_up[l].astype(BF16), w2_down=ffn2_down[l].astype(BF16))
        xp, kp, vp, cp = _layer(xp, None, None, zero_hist, lw)
        xs, kk, vv, cs = _layer(xs, cache_k[l], cache_v[l], state_conv[l], lw)
        for lst, val in zip(outs, (kp, vp, cp, kk, vv, cs)):
            lst.append(val)
    g_final = _row(norm_final)
    y_prompt = _final_norm(xp.reshape(-1, xp.shape[-1]), g_final).reshape(xp.shape)
    y_sample = _final_norm(xs.reshape(-1, xs.shape[-1]), g_final).reshape(xs.shape)
    return (y_prompt, y_sample) + tuple(jnp.stack(o) for o in outs)
```

```python
import functools

import jax
import jax.numpy as jnp
from jax import lax
from jax.experimental import pallas as pl
from jax.experimental.pallas import tpu as pltpu

F32 = jnp.float32
BF16 = jnp.bfloat16

EPS = 1e-6
SB_HEAD_DIM = 128
SB_KEY_BLOCK = 128
CONV_WIDTH = 31
HIST = CONV_WIDTH - 1
SUBLANES = 8
HIST_PAD = 32

VMEM_LIMIT = 56 * 1024 * 1024

SB_DEAD_SUM = 106.0


def _params(*sem):
    return pltpu.CompilerParams(dimension_semantics=sem, vmem_limit_bytes=VMEM_LIMIT)


def _rms(x, g):
    return x * lax.rsqrt(jnp.mean(x * x, axis=-1, keepdims=True) + EPS) * g


def _ffn_kernel(x_ref, g_ref, wg_ref, wu_ref, wd_ref, gf_ref, o_ref, h_ref, *, final_norm):
    j = pl.program_id(1)

    @pl.when(j == 0)
    def _():
        x = x_ref[...]
        h_ref[...] = _rms(x, g_ref[...]).astype(BF16)
        o_ref[...] = x

    h = h_ref[...]
    gate = jnp.dot(h, wg_ref[...], preferred_element_type=F32)
    up = jnp.dot(h, wu_ref[...], preferred_element_type=F32)
    act = (0.5 * gate * jax.nn.sigmoid(gate) * up).astype(BF16)
    o_ref[...] += jnp.dot(act, wd_ref[...], preferred_element_type=F32)

    if final_norm:
        @pl.when(j == pl.num_programs(1) - 1)
        def _():
            o_ref[...] = _rms(o_ref[...], gf_ref[...])


def _ffn(x, g, wg, wu, wd, g_final=None, *, tm=1024, tf=512):
    m, d = x.shape
    f = wg.shape[1]
    tm = min(tm, m)
    assert m % tm == 0 and f % tf == 0
    vec = pl.BlockSpec((1, d), lambda i, j: (0, 0))
    return pl.pallas_call(
        functools.partial(_ffn_kernel, final_norm=g_final is not None),
        grid=(m // tm, f // tf),
        in_specs=[
            pl.BlockSpec((tm, d), lambda i, j: (i, 0)),
            vec,
            pl.BlockSpec((d, tf), lambda i, j: (0, j)),
            pl.BlockSpec((d, tf), lambda i, j: (0, j)),
            pl.BlockSpec((tf, d), lambda i, j: (j, 0)),
            vec,
        ],
        out_specs=pl.BlockSpec((tm, d), lambda i, j: (i, 0)),
        out_shape=jax.ShapeDtypeStruct((m, d), F32),
        scratch_shapes=[pltpu.VMEM((tm, d), BF16)],
        compiler_params=_params("parallel", "arbitrary"),
        name="ffn",
    )(x, g, wg, wu, wd, g if g_final is None else g_final)


def _in_proj_kernel(x_ref, g_ref, w_ref, k_all_ref, v_all_ref, q_ref, k_ref, v_ref, kb_ref, vb_ref, a_ref,
                    *, heads):
    del k_all_ref, v_all_ref
    tm = x_ref.shape[0]
    wd = q_ref.shape[1]
    h = _rms(x_ref[...], g_ref[...]).astype(BF16)

    def group(j):
        return jnp.dot(h, w_ref[:, j * wd:(j + 1) * wd], preferred_element_type=F32)

    q_ref[...] = (group(0) * (SB_HEAD_DIM ** -0.5)).astype(BF16)
    for full_ref, half_ref, j in ((k_ref, kb_ref, 1), (v_ref, vb_ref, 2)):
        u = group(j)
        half_ref[...] = u.astype(BF16)
        for hh in range(heads):
            full_ref[pl.ds(hh, tm, stride=heads), :] = u[:, hh * SB_HEAD_DIM:(hh + 1) * SB_HEAD_DIM]
    a_ref[...] = group(3) * jax.nn.sigmoid(group(4))


def _in_proj(x, g, w, k_all, v_all, layer, *, tm=512):
    m, d = x.shape
    wd = w.shape[1] // 5
    heads = wd // SB_HEAD_DIM
    tm = min(tm, m)
    assert m % tm == 0 and w.shape[1] == 5 * wd and k_all.shape[1:] == (m * heads, SB_HEAD_DIM)
    col = pl.BlockSpec((tm, wd), lambda i: (i, 0))
    rows = pl.BlockSpec((None, tm * heads, SB_HEAD_DIM), lambda i: (layer, i, 0))
    anyspec = pl.BlockSpec(memory_space=pl.ANY)
    return pl.pallas_call(
        functools.partial(_in_proj_kernel, heads=heads),
        grid=(m // tm,),
        in_specs=[
            pl.BlockSpec((tm, d), lambda i: (i, 0)),
            pl.BlockSpec((1, d), lambda i: (0, 0)),
            pl.BlockSpec((d, 5 * wd), lambda i: (0, 0)),
            anyspec, anyspec,
        ],
        out_specs=[col, rows, rows, col, col, col],
        out_shape=[
            jax.ShapeDtypeStruct((m, wd), BF16),
            jax.ShapeDtypeStruct(k_all.shape, F32),
            jax.ShapeDtypeStruct(v_all.shape, F32),
            jax.ShapeDtypeStruct((m, wd), BF16),
            jax.ShapeDtypeStruct((m, wd), BF16),
            jax.ShapeDtypeStruct((m, wd), F32),
        ],
        input_output_aliases={3: 1, 4: 2},
        compiler_params=_params("parallel"),
        name="in_proj",
    )(x, g, w, k_all, v_all)


def _later_key_matrix(tk):
    j = lax.broadcasted_iota(jnp.int32, (tk, tk), 0)
    s = lax.broadcasted_iota(jnp.int32, (tk, tk), 1)
    return (j > s).astype(BF16)


def _strictly_earlier(tq, tk):
    t = lax.broadcasted_iota(jnp.int32, (tq, tk), 0)
    s = lax.broadcasted_iota(jnp.int32, (tq, tk), 1)
    return s < t


def _sb_chains(chains):
    logits = [[lax.dot_general(q, kb, (((1,), (1,)), ((), ())), preferred_element_type=F32)
               for kb, _, _, _ in blocks] for q, _, blocks in chains]
    staged, carries = [], []
    for (_, carry, blocks), zs in zip(chains, logits):
        row = []
        for (_, _, mask, _), z in zip(blocks, zs):
            l = jnp.log(1.0 + jnp.exp(-jnp.abs(z)))
            sp = jnp.maximum(z, 0.0) + l
            log_beta = jnp.minimum(z, 0.0) - l
            if mask is not None:
                sp = jnp.where(mask, sp, 0.0)
            hi = sp.astype(BF16)
            split = jnp.concatenate([hi, (sp - hi.astype(F32)).astype(BF16)], axis=-1)
            row.append((log_beta, split, carry))
            carry = carry + jnp.sum(sp, axis=-1, keepdims=True)
        staged.append(row)
        carries.append(carry)
    between = [[jnp.dot(split, jnp.concatenate([later, later], axis=0), preferred_element_type=F32)
                for (_, split, _), (_, _, _, later) in zip(row, blocks)]
               for row, (_, _, blocks) in zip(staged, chains)]
    weights = []
    for row, bts, (_, _, blocks) in zip(staged, between, chains):
        ws = []
        for (log_beta, _, carry), bt, (_, _, mask, _) in zip(row, bts, blocks):
            w = jnp.exp(log_beta - bt - carry)
            if mask is not None:
                w = jnp.where(mask, w, 0.0)
            ws.append(w.astype(BF16))
        weights.append(ws)
    outs = []
    for ws, (_, _, blocks) in zip(weights, chains):
        out = None
        for w, (_, vb, _, _) in zip(ws, blocks):
            pv = jnp.dot(w, vb, preferred_element_type=F32)
            out = pv if out is None else out + pv
        outs.append(out)
    return list(zip(outs, carries))


def _attn_prompt_kernel(later_ref, q_ref, k_ref, v_ref, o_ref, carry_ref, *, heads, halves, nprev):
    tb = SB_KEY_BLOCK
    i = pl.program_id(2)
    head_lanes = [slice(hh * SB_HEAD_DIM, (hh + 1) * SB_HEAD_DIM) for hh in range(heads)]
    q_rows = [slice(sub * tb, (sub + 1) * tb) for sub in range(halves)]
    diag = _strictly_earlier(tb, tb)

    def kv(blk, lanes):
        rows = pl.ds(pl.multiple_of(blk * tb, tb), tb)
        return k_ref[0, rows, lanes], v_ref[0, rows, lanes]

    def unconditional(first_step):
        later = later_ref[...]
        where, chains = [], []
        for hh, lanes in enumerate(head_lanes):
            for sub in range(halves):
                backs = range(min(sub, nprev) + 1) if first_step else range(nprev + 1)
                blocks = [kv(i * halves + sub - back, lanes) + (diag if back == 0 else None, later)
                          for back in backs]
                chains.append((q_ref[0, q_rows[sub], lanes], jnp.zeros((tb, 1), F32), blocks))
                where.append((hh, sub, lanes))
        for (hh, sub, lanes), (out, carry) in zip(where, _sb_chains(chains)):
            o_ref[0, q_rows[sub], lanes] = out
            carry_ref[hh, q_rows[sub], :] = carry

    @pl.when(i == 0)
    def _():
        unconditional(True)

    @pl.when(i > 0)
    def _():
        unconditional(False)

    def cond(state):
        n, low = state
        newest = i * halves + (halves - 1) - nprev - 1 - n
        return jnp.logical_and(newest >= 0, low < SB_DEAD_SUM)

    def body(state):
        n, _ = state
        for sub in range(halves):
            blk = i * halves + sub - nprev - 1 - n

            @pl.when(blk >= 0)
            def _():
                chains = [(q_ref[0, q_rows[sub], lanes], carry_ref[hh, q_rows[sub], :],
                           [kv(blk, lanes) + (None, later_ref[...])])
                          for hh, lanes in enumerate(head_lanes)]
                for (hh, lanes), (out, carry) in zip(enumerate(head_lanes), _sb_chains(chains)):
                    o_ref[0, q_rows[sub], lanes] += out
                    carry_ref[hh, q_rows[sub], :] = carry
        return n + 1, jnp.min(carry_ref[...])

    lax.while_loop(cond, body, (jnp.int32(0), jnp.min(carry_ref[...])))


def _attn_prompt(q, k, v, *, heads=4, halves=2, nprev=2):
    b, s, w = q.shape
    heads = min(heads, w // SB_HEAD_DIM)
    hw = heads * SB_HEAD_DIM
    tq = halves * SB_KEY_BLOCK
    assert s % tq == 0 and w % hw == 0
    kv_spec = pl.BlockSpec((1, s, hw), lambda bi, g, i: (bi, 0, g))
    q_spec = pl.BlockSpec((1, tq, hw), lambda bi, g, i: (bi, i, g))
    return pl.pallas_call(
        functools.partial(_attn_prompt_kernel, heads=heads, halves=halves, nprev=nprev),
        grid=(b, w // hw, s // tq),
        in_specs=[pl.BlockSpec((SB_KEY_BLOCK, SB_KEY_BLOCK), lambda bi, g, i: (0, 0)), q_spec, kv_spec, kv_spec],
        out_specs=q_spec,
        out_shape=jax.ShapeDtypeStruct((b, s, w), F32),
        scratch_shapes=[pltpu.VMEM((heads, tq, 1), F32)],
        compiler_params=_params("parallel", "parallel", "arbitrary"),
        name="attn_prompt",
    )(_later_key_matrix(SB_KEY_BLOCK), q, k, v)


def _attn_sample_kernel(later_ref, q_ref, kn_ref, vn_ref, kw_ref, vw_ref, kc_hbm, vc_hbm, o_ref,
                        carry_ref, kbuf, vbuf, sem, *, layer, heads, window):
    tb = SB_KEY_BLOCK
    b = pl.program_id(0)
    t = q_ref.shape[1]
    n_past = kc_hbm.shape[2] // (heads * tb)
    head_lanes = [slice(hh * SB_HEAD_DIM, (hh + 1) * SB_HEAD_DIM) for hh in range(heads)]

    def head_kv(k_rows_ref, v_rows_ref, blk, hh):
        rows = pl.ds(blk * tb * heads + hh, tb, stride=heads)
        return k_rows_ref[rows, :].astype(BF16), v_rows_ref[rows, :].astype(BF16)

    later = later_ref[...]
    chains = []
    for hh, lanes in enumerate(head_lanes):
        blocks = [(kn_ref[0, :, lanes], vn_ref[0, :, lanes], _strictly_earlier(t, t), later_ref[:t, :t])]
        blocks += [head_kv(kw_ref, vw_ref, blk, hh) + (None, later) for blk in range(window - 1, -1, -1)]
        chains.append((q_ref[0, :, lanes], jnp.zeros((t, 1), F32), blocks))
    for (hh, lanes), (out, carry) in zip(enumerate(head_lanes), _sb_chains(chains)):
        o_ref[0, :, lanes] = out
        carry_ref[hh] = carry

    def cond(state):
        j, low = state
        return jnp.logical_and(j >= 0, low < SB_DEAD_SUM)

    def body(state):
        j, _ = state
        rows = pl.ds(pl.multiple_of(j * (tb * heads), tb * heads), tb * heads)
        copies = [pltpu.make_async_copy(src.at[layer, b, rows, :], dst, sem.at[n])
                  for n, (src, dst) in enumerate(((kc_hbm, kbuf), (vc_hbm, vbuf)))]
        for cp in copies:
            cp.start()
        for cp in copies:
            cp.wait()
        chains = [(q_ref[0, :, lanes], carry_ref[hh], [head_kv(kbuf, vbuf, 0, hh) + (None, later_ref[...])])
                  for hh, lanes in enumerate(head_lanes)]
        for (hh, lanes), (out, carry) in zip(enumerate(head_lanes), _sb_chains(chains)):
            o_ref[0, :, lanes] += out
            carry_ref[hh] = carry
        return j - 1, jnp.min(carry_ref[...])

    lax.while_loop(cond, body, (jnp.int32(n_past - window - 1), jnp.min(carry_ref[...])))


def _attn_sample(q, k_new, v_new, k_cache, v_cache, layer, *, window=2):
    b, t, w = q.shape
    heads = w // SB_HEAD_DIM
    blk_rows = SB_KEY_BLOCK * heads
    n_past = k_cache.shape[2] // blk_rows
    assert k_cache.shape[2] == n_past * blk_rows and n_past % window == 0 and t <= SB_KEY_BLOCK
    new_spec = pl.BlockSpec((1, t, w), lambda bi: (bi, 0, 0))
    win_spec = pl.BlockSpec((None, None, window * blk_rows, SB_HEAD_DIM),
                            lambda bi: (layer, bi, n_past // window - 1, 0))
    anyspec = pl.BlockSpec(memory_space=pl.ANY)
    return pl.pallas_call(
        functools.partial(_attn_sample_kernel, layer=layer, heads=heads, window=window),
        grid=(b,),
        in_specs=[pl.BlockSpec((SB_KEY_BLOCK, SB_KEY_BLOCK), lambda bi: (0, 0)),
                  new_spec, new_spec, new_spec, win_spec, win_spec, anyspec, anyspec],
        out_specs=new_spec,
        out_shape=jax.ShapeDtypeStruct((b, t, w), F32),
        scratch_shapes=[pltpu.VMEM((heads, t, 1), F32),
                        pltpu.VMEM((blk_rows, SB_HEAD_DIM), F32), pltpu.VMEM((blk_rows, SB_HEAD_DIM), F32),
                        pltpu.SemaphoreType.DMA((2,))],
        compiler_params=_params("parallel"),
        name="attn_sample",
    )(_later_key_matrix(SB_KEY_BLOCK), q, k_new, v_new, k_cache, v_cache, k_cache, v_cache)


def _conv_kernel(a_ref, prev_ref, hist_ref, dw_ref, db_ref, lg_ref, lb_ref, go_ref, o_ref,
                 sh_ref, *, tt, rows):
    i = pl.program_id(1)
    c = a_ref.shape[2]

    @pl.when(i == 0)
    def _():
        sh_ref[0, 0:SUBLANES, :] = jnp.zeros((SUBLANES, c), F32)
        sh_ref[0, HIST_PAD - HIST:HIST_PAD, :] = hist_ref[0]

    @pl.when(i > 0)
    def _():
        sh_ref[0, 0:HIST_PAD, :] = prev_ref[0]

    sh_ref[0, HIST_PAD:HIST_PAD + tt, :] = a_ref[0]

    n_shift = tt + HIST_PAD - SUBLANES
    for b in range(1, SUBLANES):
        sh_ref[b, 0:n_shift, :] = sh_ref[0, b:b + n_shift, :]

    w0 = HIST_PAD - HIST
    groups = rows // SUBLANES
    for r0 in range(0, tt, rows):
        ys = [None] * groups
        for w in range(w0, w0 + CONV_WIDTH):
            tap = dw_ref[w - w0]
            for gi in range(groups):
                base = r0 + gi * SUBLANES + (w // SUBLANES) * SUBLANES
                term = sh_ref[w % SUBLANES, base:base + SUBLANES, :] * tap
                ys[gi] = term if ys[gi] is None else ys[gi] + term
        y = jnp.concatenate(ys, axis=0) + db_ref[...]
        mu = jnp.mean(y, axis=-1, keepdims=True)
        yc = y - mu
        var = jnp.mean(yc * yc, axis=-1, keepdims=True)
        y = yc * lax.rsqrt(var + EPS) * lg_ref[...] + lb_ref[...]
        y = y * jax.nn.sigmoid(y)
        o_ref[0, r0:r0 + rows, :] = _rms(y, go_ref[...]).astype(o_ref.dtype)


def _conv(a, hist, dw_w, dw_b, ln_g, ln_b, g_out, *, tt=256, rows=32):
    b, t, c = a.shape
    tt = min(tt, t)
    assert t % tt == 0 and tt % rows == 0 and tt % HIST_PAD == 0 and rows % SUBLANES == 0
    per = tt // HIST_PAD
    vec = pl.BlockSpec((1, c), lambda bi, i: (0, 0))
    taps = jnp.broadcast_to(dw_w[:, None, :], (CONV_WIDTH, SUBLANES, c))
    return pl.pallas_call(
        functools.partial(_conv_kernel, tt=tt, rows=rows),
        grid=(b, t // tt),
        in_specs=[
            pl.BlockSpec((1, tt, c), lambda bi, i: (bi, i, 0)),
            pl.BlockSpec((1, HIST_PAD, c), lambda bi, i: (bi, jnp.maximum(i * per - 1, 0), 0)),
            pl.BlockSpec((1, HIST, c), lambda bi, i: (bi, 0, 0)),
            pl.BlockSpec((CONV_WIDTH, SUBLANES, c), lambda bi, i: (0, 0, 0)),
            vec, vec, vec, vec,
        ],
        out_specs=pl.BlockSpec((1, tt, c), lambda bi, i: (bi, i, 0)),
        out_shape=jax.ShapeDtypeStruct((b, t, c), BF16),
        scratch_shapes=[pltpu.VMEM((SUBLANES, HIST_PAD + tt, c), F32)],
        compiler_params=_params("parallel", "arbitrary"),
        name="conv",
    )(a, a, hist, taps, dw_b, ln_g, ln_b, g_out)


def _out_proj_kernel(x_ref, attn_ref, conv_ref, ga_ref, w_ref, o_ref):
    sb = attn_ref.shape[1]
    attn_n = _rms(attn_ref[...], ga_ref[...]).astype(BF16)
    o_ref[...] = (x_ref[...]
                  + jnp.dot(attn_n, w_ref[:sb, :], preferred_element_type=F32)
                  + jnp.dot(conv_ref[...], w_ref[sb:, :], preferred_element_type=F32))


def _out_proj(x, attn, conv_n, g_attn, w, *, tm=512):
    m, d = x.shape
    sb, c = attn.shape[1], conv_n.shape[1]
    tm = min(tm, m)
    assert m % tm == 0 and w.shape == (sb + c, d)
    return pl.pallas_call(
        _out_proj_kernel,
        grid=(m // tm,),
        in_specs=[
            pl.BlockSpec((tm, d), lambda i: (i, 0)),
            pl.BlockSpec((tm, sb), lambda i: (i, 0)),
            pl.BlockSpec((tm, c), lambda i: (i, 0)),
            pl.BlockSpec((1, sb), lambda i: (0, 0)),
            pl.BlockSpec((sb + c, d), lambda i: (0, 0)),
        ],
        out_specs=pl.BlockSpec((tm, d), lambda i: (i, 0)),
        out_shape=jax.ShapeDtypeStruct((m, d), F32),
        compiler_params=_params("parallel"),
        name="out_proj",
    )(x, attn, conv_n, g_attn, w)


def _row(v):
    return v.reshape(1, -1).astype(F32)


def _layer(x, k_all, v_all, layer, cache, conv_hist, lw, g_final):
    b, t, d = x.shape
    x2 = x.reshape(b * t, d)
    x2 = _ffn(x2, lw["g_ffn1"], lw["w1_gate"], lw["w1_up"], lw["w1_down"])
    q, k_all, v_all, kb, vb, a = _in_proj(x2, lw["g_mix"], lw["w_in"], k_all, v_all, layer)
    sbw = q.shape[1]
    seq = lambda y: y.reshape(b, t, y.shape[1])
    if cache is None:
        attn = _attn_prompt(seq(q), seq(kb), seq(vb))
    else:
        attn = _attn_sample(seq(q), seq(kb), seq(vb), cache[0], cache[1], layer)
    a3 = seq(a)
    conv_n = _conv(a3, conv_hist, lw["dw_w"], lw["dw_b"], lw["ln_g"], lw["ln_b"], lw["g_conv_out"])
    x2 = _out_proj(x2, attn.reshape(b * t, sbw), conv_n.reshape(b * t, -1), lw["g_attn_out"], lw["w_out"])
    x2 = _ffn(x2, lw["g_ffn2"], lw["w2_gate"], lw["w2_up"], lw["w2_down"], g_final)
    return x2.reshape(b, t, d), k_all, v_all, a3[:, t - HIST:, :]


def kernel(x_prompt, x_sample, cache_k, cache_v, state_conv, norm_ffn1, ffn1_gate, ffn1_up, ffn1_down, norm_mix, w_in, dw_weight, dw_bias, conv_ln_gain, conv_ln_bias, norm_attn_out, norm_conv_out, w_out, norm_ffn2, ffn2_gate, ffn2_up, ffn2_down, norm_final):
    depth = w_in.shape[0]
    conv_ch = dw_weight.shape[2]
    heads, hd = cache_k.shape[3], cache_k.shape[4]
    assert hd == SB_HEAD_DIM and x_prompt.shape[1] >= HIST and x_sample.shape[1] >= HIST
    (bp, sp, _), (bs, ts, _) = x_prompt.shape, x_sample.shape
    past = cache_k.shape[2]
    xp, xs = x_prompt, x_sample
    zero_hist = jnp.zeros((bp, HIST, conv_ch), F32)
    kp = jnp.zeros((depth, bp * sp * heads, hd), F32)
    vp = jnp.zeros((depth, bp * sp * heads, hd), F32)
    ks = jnp.zeros((depth, bs * ts * heads, hd), F32)
    vs = jnp.zeros((depth, bs * ts * heads, hd), F32)
    cache = (cache_k.reshape(depth, bs, past * heads, hd), cache_v.reshape(depth, bs, past * heads, hd))
    g_final = _row(norm_final)
    conv_p, conv_s = [], []
    for l in range(depth):
        lw = dict(
            g_ffn1=_row(norm_ffn1[l]), w1_gate=ffn1_gate[l].astype(BF16), w1_up=ffn1_up[l].astype(BF16),
            w1_down=ffn1_down[l].astype(BF16), g_mix=_row(norm_mix[l]), w_in=w_in[l].astype(BF16),
            dw_w=dw_weight[l].astype(F32), dw_b=_row(dw_bias[l]), ln_g=_row(conv_ln_gain[l]),
            ln_b=_row(conv_ln_bias[l]), g_attn_out=_row(norm_attn_out[l]), g_conv_out=_row(norm_conv_out[l]),
            w_out=w_out[l].astype(BF16), g_ffn2=_row(norm_ffn2[l]), w2_gate=ffn2_gate[l].astype(BF16),
            w2_up=ffn2_up[l].astype(BF16), w2_down=ffn2_down[l].astype(BF16))
        closing = g_final if l == depth - 1 else None
        xp, kp, vp, cp = _layer(xp, kp, vp, l, None, zero_hist, lw, closing)
        xs, ks, vs, cs = _layer(xs, ks, vs, l, cache, state_conv[l], lw, closing)
        conv_p.append(cp)
        conv_s.append(cs)
    return (xp, xs,
            kp.reshape(depth, bp, sp, heads, hd), vp.reshape(depth, bp, sp, heads, hd), jnp.stack(conv_p),
            ks.reshape(depth, bs, ts, heads, hd), vs.reshape(depth, bs, ts, heads, hd), jnp.stack(conv_s))
```

```python
import functools

import jax
import jax.numpy as jnp
from jax import lax
from jax.experimental import pallas as pl
from jax.experimental.pallas import tpu as pltpu

F32 = jnp.float32
BF16 = jnp.bfloat16

EPS = 1e-6
SB_HEAD_DIM = 128
SB_KEY_BLOCK = 128
CONV_WIDTH = 31
HIST = CONV_WIDTH - 1
SUBLANES = 8
HIST_PAD = 32

VMEM_LIMIT = 56 * 1024 * 1024

SB_DEAD_SUM = 106.0


def _params(*sem):
    return pltpu.CompilerParams(dimension_semantics=sem, vmem_limit_bytes=VMEM_LIMIT)


def _rms(x, g):
    return x * lax.rsqrt(jnp.mean(x * x, axis=-1, keepdims=True) + EPS) * g


def _ffn_kernel(x_ref, g_ref, wg_ref, wu_ref, wd_ref, gf_ref, o_ref, h_ref, *, final_norm):
    j = pl.program_id(1)

    @pl.when(j == 0)
    def _():
        x = x_ref[...]
        h_ref[...] = _rms(x, g_ref[...]).astype(BF16)
        o_ref[...] = x

    h = h_ref[...]
    gate = jnp.dot(h, wg_ref[...], preferred_element_type=F32)
    up = jnp.dot(h, wu_ref[...], preferred_element_type=F32)
    act = (0.5 * gate * jax.nn.sigmoid(gate) * up).astype(BF16)
    o_ref[...] += jnp.dot(act, wd_ref[...], preferred_element_type=F32)

    if final_norm:
        @pl.when(j == pl.num_programs(1) - 1)
        def _():
            o_ref[...] = _rms(o_ref[...], gf_ref[...])


def _ffn(x, g, wg, wu, wd, g_final=None, *, tm=1024, tf=512):
    m, d = x.shape
    f = wg.shape[1]
    tm = min(tm, m)
    assert m % tm == 0 and f % tf == 0
    vec = pl.BlockSpec((1, d), lambda i, j: (0, 0))
    return pl.pallas_call(
        functools.partial(_ffn_kernel, final_norm=g_final is not None),
        grid=(m // tm, f // tf),
        in_specs=[
            pl.BlockSpec((tm, d), lambda i, j: (i, 0)),
            vec,
            pl.BlockSpec((d, tf), lambda i, j: (0, j)),
            pl.BlockSpec((d, tf), lambda i, j: (0, j)),
            pl.BlockSpec((tf, d), lambda i, j: (j, 0)),
            vec,
        ],
        out_specs=pl.BlockSpec((tm, d), lambda i, j: (i, 0)),
        out_shape=jax.ShapeDtypeStruct((m, d), F32),
        scratch_shapes=[pltpu.VMEM((tm, d), BF16)],
        compiler_params=_params("parallel", "arbitrary"),
        name="ffn",
    )(x, g, wg, wu, wd, g if g_final is None else g_final)


def _conv_rows(src_ref, src0, sh_ref, dw_ref, db_ref, lg_ref, lb_ref, go_ref, o_ref, out0, *, sub, rows):
    n_shift = sub + HIST_PAD - SUBLANES
    for b in range(1, SUBLANES):
        sh_ref[b, 0:n_shift, :] = src_ref[src0 + b:src0 + b + n_shift, :]
    w0 = HIST_PAD - HIST
    groups = rows // SUBLANES
    for r0 in range(0, sub, rows):
        ys = [None] * groups
        for w in range(w0, w0 + CONV_WIDTH):
            tap = dw_ref[w - w0]
            for gi in range(groups):
                base = r0 + gi * SUBLANES + (w // SUBLANES) * SUBLANES
                if w % SUBLANES == 0:
                    frames = src_ref[src0 + base:src0 + base + SUBLANES, :]
                else:
                    frames = sh_ref[w % SUBLANES, base:base + SUBLANES, :]
                term = frames * tap
                ys[gi] = term if ys[gi] is None else ys[gi] + term
        y = jnp.concatenate(ys, axis=0) + db_ref[...]
        mu = jnp.mean(y, axis=-1, keepdims=True)
        yc = y - mu
        var = jnp.mean(yc * yc, axis=-1, keepdims=True)
        y = yc * lax.rsqrt(var + EPS) * lg_ref[...] + lb_ref[...]
        y = y * jax.nn.sigmoid(y)
        o_ref[out0 + r0:out0 + r0 + rows, :] = _rms(y, go_ref[...]).astype(o_ref.dtype)


def _start_history(a_ref, hist_ref):
    a_ref[0:SUBLANES, :] = jnp.zeros((SUBLANES, a_ref.shape[1]), F32)
    a_ref[HIST_PAD - HIST:HIST_PAD, :] = hist_ref[0]


def _conv_kernel(a_ref, prev_ref, hist_ref, dw_ref, db_ref, lg_ref, lb_ref, go_ref, o_ref,
                 full_ref, sh_ref, *, tt, rows):
    i = pl.program_id(1)

    @pl.when(i == 0)
    def _():
        _start_history(full_ref, hist_ref)

    @pl.when(i > 0)
    def _():
        full_ref[0:HIST_PAD, :] = prev_ref[0]

    full_ref[HIST_PAD:HIST_PAD + tt, :] = a_ref[0]
    _conv_rows(full_ref, 0, sh_ref, dw_ref, db_ref, lg_ref, lb_ref, go_ref, o_ref.at[0], 0, sub=tt, rows=rows)


def _conv_taps(dw_w):
    return jnp.broadcast_to(dw_w[:, None, :], (CONV_WIDTH, SUBLANES, dw_w.shape[1]))


def _conv(a, hist, dw_w, dw_b, ln_g, ln_b, g_out, *, tt=256, rows=32):
    b, t, c = a.shape
    tt = min(tt, t)
    assert t % tt == 0 and tt % rows == 0 and tt % HIST_PAD == 0 and rows % SUBLANES == 0
    per = tt // HIST_PAD
    vec = pl.BlockSpec((1, c), lambda bi, i: (0, 0))
    return pl.pallas_call(
        functools.partial(_conv_kernel, tt=tt, rows=rows),
        grid=(b, t // tt),
        in_specs=[
            pl.BlockSpec((1, tt, c), lambda bi, i: (bi, i, 0)),
            pl.BlockSpec((1, HIST_PAD, c), lambda bi, i: (bi, jnp.maximum(i * per - 1, 0), 0)),
            pl.BlockSpec((1, HIST, c), lambda bi, i: (bi, 0, 0)),
            pl.BlockSpec((CONV_WIDTH, SUBLANES, c), lambda bi, i: (0, 0, 0)),
            vec, vec, vec, vec,
        ],
        out_specs=pl.BlockSpec((1, tt, c), lambda bi, i: (bi, i, 0)),
        out_shape=jax.ShapeDtypeStruct((b, t, c), BF16),
        scratch_shapes=[pltpu.VMEM((HIST_PAD + tt, c), F32), pltpu.VMEM((SUBLANES, HIST_PAD + tt, c), F32)],
        compiler_params=_params("parallel", "arbitrary"),
        name="conv",
    )(a, a, hist, _conv_taps(dw_w), dw_b, ln_g, ln_b, g_out)


def _project_qkv(h, w_ref, q_ref, k_ref, v_ref, kb_ref, vb_ref, heads):
    tm = h.shape[0]
    wd = q_ref.shape[1]

    def group(j):
        return jnp.dot(h, w_ref[:, j * wd:(j + 1) * wd], preferred_element_type=F32)

    q_ref[...] = (group(0) * (SB_HEAD_DIM ** -0.5)).astype(BF16)
    for full_ref, half_ref, j in ((k_ref, kb_ref, 1), (v_ref, vb_ref, 2)):
        u = group(j)
        half_ref[...] = u.astype(BF16)
        for hh in range(heads):
            full_ref[pl.ds(hh, tm, stride=heads), :] = u[:, hh * SB_HEAD_DIM:(hh + 1) * SB_HEAD_DIM]


def _project_glu(h, w_ref, wd):
    value = jnp.dot(h, w_ref[:, 3 * wd:4 * wd], preferred_element_type=F32)
    gate = jnp.dot(h, w_ref[:, 4 * wd:5 * wd], preferred_element_type=F32)
    return value * jax.nn.sigmoid(gate)


def _in_proj_kernel(x_ref, g_ref, w_ref, k_all_ref, v_all_ref, q_ref, k_ref, v_ref, kb_ref, vb_ref, a_ref,
                    *, heads):
    del k_all_ref, v_all_ref
    h = _rms(x_ref[...], g_ref[...]).astype(BF16)
    a_ref[...] = _project_glu(h, w_ref, q_ref.shape[1])
    _project_qkv(h, w_ref, q_ref, k_ref, v_ref, kb_ref, vb_ref, heads)


def _in_proj_conv_kernel(x_ref, g_ref, w_ref, k_all_ref, v_all_ref, hist_ref, dw_ref, db_ref, lg_ref, lb_ref,
                         go_ref, q_ref, k_ref, v_ref, kb_ref, vb_ref, c_ref, hist_out_ref, a_ref, sh_ref,
                         *, heads, tiles_per_seq, sub, rows):
    del k_all_ref, v_all_ref
    tm = x_ref.shape[0]
    first = pl.program_id(0) % tiles_per_seq == 0

    @pl.when(first)
    def _():
        _start_history(a_ref, hist_ref)

    @pl.when(jnp.logical_not(first))
    def _():
        a_ref[0:HIST_PAD, :] = a_ref[tm:tm + HIST_PAD, :]

    h = _rms(x_ref[...], g_ref[...]).astype(BF16)
    a = _project_glu(h, w_ref, q_ref.shape[1])
    a_ref[HIST_PAD:HIST_PAD + tm, :] = a
    hist_out_ref[0] = a[tm - HIST:, :]
    _project_qkv(h, w_ref, q_ref, k_ref, v_ref, kb_ref, vb_ref, heads)
    for s0 in range(0, tm, sub):
        _conv_rows(a_ref, s0, sh_ref, dw_ref, db_ref, lg_ref, lb_ref, go_ref, c_ref, s0, sub=sub, rows=rows)


def _in_proj_outs(m, wd, k_all, v_all, last_dtype):
    return [
        jax.ShapeDtypeStruct((m, wd), BF16),
        jax.ShapeDtypeStruct(k_all.shape, F32),
        jax.ShapeDtypeStruct(v_all.shape, F32),
        jax.ShapeDtypeStruct((m, wd), BF16),
        jax.ShapeDtypeStruct((m, wd), BF16),
        jax.ShapeDtypeStruct((m, wd), last_dtype),
    ]


def _in_proj(x, g, w, k_all, v_all, layer, *, tm=512):
    m, d = x.shape
    wd = w.shape[1] // 5
    heads = wd // SB_HEAD_DIM
    tm = min(tm, m)
    assert m % tm == 0 and w.shape[1] == 5 * wd and k_all.shape[1:] == (m * heads, SB_HEAD_DIM)
    col = pl.BlockSpec((tm, wd), lambda i: (i, 0))
    rows = pl.BlockSpec((None, tm * heads, SB_HEAD_DIM), lambda i: (layer, i, 0))
    anyspec = pl.BlockSpec(memory_space=pl.ANY)
    return pl.pallas_call(
        functools.partial(_in_proj_kernel, heads=heads),
        grid=(m // tm,),
        in_specs=[
            pl.BlockSpec((tm, d), lambda i: (i, 0)),
            pl.BlockSpec((1, d), lambda i: (0, 0)),
            pl.BlockSpec((d, 5 * wd), lambda i: (0, 0)),
            anyspec, anyspec,
        ],
        out_specs=[col, rows, rows, col, col, col],
        out_shape=_in_proj_outs(m, wd, k_all, v_all, F32),
        input_output_aliases={3: 1, 4: 2},
        compiler_params=_params("parallel"),
        name="in_proj",
    )(x, g, w, k_all, v_all)


def _in_proj_conv(x, g, w, k_all, v_all, layer, seq_len, hist, dw_w, dw_b, ln_g, ln_b, g_out,
                  *, tm=256, sub=128, rows=32):
    m, d = x.shape
    wd = w.shape[1] // 5
    heads = wd // SB_HEAD_DIM
    n_seq = m // seq_len
    assert seq_len % tm == 0 and tm % sub == 0 and sub % rows == 0 and rows % SUBLANES == 0 and tm >= HIST_PAD
    assert w.shape[1] == 5 * wd and k_all.shape[1:] == (m * heads, SB_HEAD_DIM) and hist.shape == (n_seq, HIST, wd)
    tiles_per_seq = seq_len // tm
    col = pl.BlockSpec((tm, wd), lambda i: (i, 0))
    krows = pl.BlockSpec((None, tm * heads, SB_HEAD_DIM), lambda i: (layer, i, 0))
    hist_spec = pl.BlockSpec((1, HIST, wd), lambda i: (i // tiles_per_seq, 0, 0))
    vec = pl.BlockSpec((1, wd), lambda i: (0, 0))
    anyspec = pl.BlockSpec(memory_space=pl.ANY)
    return pl.pallas_call(
        functools.partial(_in_proj_conv_kernel, heads=heads, tiles_per_seq=tiles_per_seq, sub=sub, rows=rows),
        grid=(m // tm,),
        in_specs=[
            pl.BlockSpec((tm, d), lambda i: (i, 0)),
            pl.BlockSpec((1, d), lambda i: (0, 0)),
            pl.BlockSpec((d, 5 * wd), lambda i: (0, 0)),
            anyspec, anyspec,
            hist_spec,
            pl.BlockSpec((CONV_WIDTH, SUBLANES, wd), lambda i: (0, 0, 0)),
            vec, vec, vec, vec,
        ],
        out_specs=[col, krows, krows, col, col, col, hist_spec],
        out_shape=_in_proj_outs(m, wd, k_all, v_all, BF16) + [jax.ShapeDtypeStruct((n_seq, HIST, wd), F32)],
        input_output_aliases={3: 1, 4: 2},
        scratch_shapes=[pltpu.VMEM((HIST_PAD + tm, wd), F32), pltpu.VMEM((SUBLANES, HIST_PAD + sub, wd), F32)],
        compiler_params=_params("arbitrary"),
        name="in_proj_conv",
    )(x, g, w, k_all, v_all, hist, _conv_taps(dw_w), dw_b, ln_g, ln_b, g_out)


def _later_key_matrix(tk):
    j = lax.broadcasted_iota(jnp.int32, (tk, tk), 0)
    s = lax.broadcasted_iota(jnp.int32, (tk, tk), 1)
    return (j > s).astype(BF16)


def _strictly_earlier(tq, tk):
    t = lax.broadcasted_iota(jnp.int32, (tq, tk), 0)
    s = lax.broadcasted_iota(jnp.int32, (tq, tk), 1)
    return s < t


def _sb_chains(chains):
    logits = [[lax.dot_general(q, kb, (((1,), (1,)), ((), ())), preferred_element_type=F32)
               for kb, _, _, _ in blocks] for q, _, blocks in chains]
    staged, carries = [], []
    for (_, carry, blocks), zs in zip(chains, logits):
        row = []
        for (_, _, mask, _), z in zip(blocks, zs):
            l = jnp.log(1.0 + jnp.exp(-jnp.abs(z)))
            sp = jnp.maximum(z, 0.0) + l
            log_beta = jnp.minimum(z, 0.0) - l
            if mask is not None:
                sp = jnp.where(mask, sp, 0.0)
            hi = sp.astype(BF16)
            split = jnp.concatenate([hi, (sp - hi.astype(F32)).astype(BF16)], axis=-1)
            row.append((log_beta, split, carry))
            carry = carry + jnp.sum(sp, axis=-1, keepdims=True)
        staged.append(row)
        carries.append(carry)
    between = [[jnp.dot(split, jnp.concatenate([later, later], axis=0), preferred_element_type=F32)
                for (_, split, _), (_, _, _, later) in zip(row, blocks)]
               for row, (_, _, blocks) in zip(staged, chains)]
    weights = []
    for row, bts, (_, _, blocks) in zip(staged, between, chains):
        ws = []
        for (log_beta, _, carry), bt, (_, _, mask, _) in zip(row, bts, blocks):
            w = jnp.exp(log_beta - bt - carry)
            if mask is not None:
                w = jnp.where(mask, w, 0.0)
            ws.append(w.astype(BF16))
        weights.append(ws)
    outs = []
    for ws, (_, _, blocks) in zip(weights, chains):
        out = None
        for w, (_, vb, _, _) in zip(ws, blocks):
            pv = jnp.dot(w, vb, preferred_element_type=F32)
            out = pv if out is None else out + pv
        outs.append(out)
    return list(zip(outs, carries))


def _attn_prompt_kernel(later_ref, q_ref, k_ref, v_ref, o_ref, carry_ref, *, heads, halves, nprev):
    tb = SB_KEY_BLOCK
    i = pl.program_id(2)
    head_lanes = [slice(hh * SB_HEAD_DIM, (hh + 1) * SB_HEAD_DIM) for hh in range(heads)]
    q_rows = [slice(sub * tb, (sub + 1) * tb) for sub in range(halves)]
    diag = _strictly_earlier(tb, tb)

    def kv(blk, lanes):
        rows = pl.ds(pl.multiple_of(blk * tb, tb), tb)
        return k_ref[0, rows, lanes], v_ref[0, rows, lanes]

    def unconditional(first_step):
        later = later_ref[...]
        where, chains = [], []
        for hh, lanes in enumerate(head_lanes):
            for sub in range(halves):
                backs = range(min(sub, nprev) + 1) if first_step else range(nprev + 1)
                blocks = [kv(i * halves + sub - back, lanes) + (diag if back == 0 else None, later)
                          for back in backs]
                chains.append((q_ref[0, q_rows[sub], lanes], jnp.zeros((tb, 1), F32), blocks))
                where.append((hh, sub, lanes))
        for (hh, sub, lanes), (out, carry) in zip(where, _sb_chains(chains)):
            o_ref[0, q_rows[sub], lanes] = out
            carry_ref[hh, q_rows[sub], :] = carry

    @pl.when(i == 0)
    def _():
        unconditional(True)

    @pl.when(i > 0)
    def _():
        unconditional(False)

    def cond(state):
        n, low = state
        newest = i * halves + (halves - 1) - nprev - 1 - n
        return jnp.logical_and(newest >= 0, low < SB_DEAD_SUM)

    def body(state):
        n, _ = state
        for sub in range(halves):
            blk = i * halves + sub - nprev - 1 - n

            @pl.when(blk >= 0)
            def _():
                chains = [(q_ref[0, q_rows[sub], lanes], carry_ref[hh, q_rows[sub], :],
                           [kv(blk, lanes) + (None, later_ref[...])])
                          for hh, lanes in enumerate(head_lanes)]
                for (hh, lanes), (out, carry) in zip(enumerate(head_lanes), _sb_chains(chains)):
                    o_ref[0, q_rows[sub], lanes] += out
                    carry_ref[hh, q_rows[sub], :] = carry
        return n + 1, jnp.min(carry_ref[...])

    lax.while_loop(cond, body, (jnp.int32(0), jnp.min(carry_ref[...])))


def _attn_prompt(q, k, v, *, heads=4, halves=4, nprev=2):
    b, s, w = q.shape
    heads = min(heads, w // SB_HEAD_DIM)
    hw = heads * SB_HEAD_DIM
    tq = halves * SB_KEY_BLOCK
    assert s % tq == 0 and w % hw == 0
    kv_spec = pl.BlockSpec((1, s, hw), lambda bi, g, i: (bi, 0, g))
    q_spec = pl.BlockSpec((1, tq, hw), lambda bi, g, i: (bi, i, g))
    return pl.pallas_call(
        functools.partial(_attn_prompt_kernel, heads=heads, halves=halves, nprev=nprev),
        grid=(b, w // hw, s // tq),
        in_specs=[pl.BlockSpec((SB_KEY_BLOCK, SB_KEY_BLOCK), lambda bi, g, i: (0, 0)), q_spec, kv_spec, kv_spec],
        out_specs=q_spec,
        out_shape=jax.ShapeDtypeStruct((b, s, w), F32),
        scratch_shapes=[pltpu.VMEM((heads, tq, 1), F32)],
        compiler_params=_params("parallel", "parallel", "arbitrary"),
        name="attn_prompt",
    )(_later_key_matrix(SB_KEY_BLOCK), q, k, v)


def _attn_sample_kernel(later_ref, q_ref, kn_ref, vn_ref, kw_ref, vw_ref, kc_hbm, vc_hbm, o_ref,
                        carry_ref, kbuf, vbuf, sem, *, layer, heads, window):
    tb = SB_KEY_BLOCK
    b = pl.program_id(0)
    t = q_ref.shape[1]
    n_past = kc_hbm.shape[2] // (heads * tb)
    head_lanes = [slice(hh * SB_HEAD_DIM, (hh + 1) * SB_HEAD_DIM) for hh in range(heads)]

    def head_kv(k_rows_ref, v_rows_ref, blk, hh):
        rows = pl.ds(blk * tb * heads + hh, tb, stride=heads)
        return k_rows_ref[rows, :].astype(BF16), v_rows_ref[rows, :].astype(BF16)

    later = later_ref[...]
    chains = []
    for hh, lanes in enumerate(head_lanes):
        blocks = [(kn_ref[0, :, lanes], vn_ref[0, :, lanes], _strictly_earlier(t, t), later_ref[:t, :t])]
        blocks += [head_kv(kw_ref, vw_ref, blk, hh) + (None, later) for blk in range(window - 1, -1, -1)]
        chains.append((q_ref[0, :, lanes], jnp.zeros((t, 1), F32), blocks))
    for (hh, lanes), (out, carry) in zip(enumerate(head_lanes), _sb_chains(chains)):
        o_ref[0, :, lanes] = out
        carry_ref[hh] = carry

    def cond(state):
        j, low = state
        return jnp.logical_and(j >= 0, low < SB_DEAD_SUM)

    def body(state):
        j, _ = state
        rows = pl.ds(pl.multiple_of(j * (tb * heads), tb * heads), tb * heads)
        copies = [pltpu.make_async_copy(src.at[layer, b, rows, :], dst, sem.at[n])
                  for n, (src, dst) in enumerate(((kc_hbm, kbuf), (vc_hbm, vbuf)))]
        for cp in copies:
            cp.start()
        for cp in copies:
            cp.wait()
        chains = [(q_ref[0, :, lanes], carry_ref[hh], [head_kv(kbuf, vbuf, 0, hh) + (None, later_ref[...])])
                  for hh, lanes in enumerate(head_lanes)]
        for (hh, lanes), (out, carry) in zip(enumerate(head_lanes), _sb_chains(chains)):
            o_ref[0, :, lanes] += out
            carry_ref[hh] = carry
        return j - 1, jnp.min(carry_ref[...])

    lax.while_loop(cond, body, (jnp.int32(n_past - window - 1), jnp.min(carry_ref[...])))


def _attn_sample(q, k_new, v_new, k_cache, v_cache, layer, *, window=2):
    b, t, w = q.shape
    heads = w // SB_HEAD_DIM
    blk_rows = SB_KEY_BLOCK * heads
    n_past = k_cache.shape[2] // blk_rows
    assert k_cache.shape[2] == n_past * blk_rows and n_past % window == 0 and t <= SB_KEY_BLOCK
    new_spec = pl.BlockSpec((1, t, w), lambda bi: (bi, 0, 0))
    win_spec = pl.BlockSpec((None, None, window * blk_rows, SB_HEAD_DIM),
                            lambda bi: (layer, bi, n_past // window - 1, 0))
    anyspec = pl.BlockSpec(memory_space=pl.ANY)
    return pl.pallas_call(
        functools.partial(_attn_sample_kernel, layer=layer, heads=heads, window=window),
        grid=(b,),
        in_specs=[pl.BlockSpec((SB_KEY_BLOCK, SB_KEY_BLOCK), lambda bi: (0, 0)),
                  new_spec, new_spec, new_spec, win_spec, win_spec, anyspec, anyspec],
        out_specs=new_spec,
        out_shape=jax.ShapeDtypeStruct((b, t, w), F32),
        scratch_shapes=[pltpu.VMEM((heads, t, 1), F32),
                        pltpu.VMEM((blk_rows, SB_HEAD_DIM), F32), pltpu.VMEM((blk_rows, SB_HEAD_DIM), F32),
                        pltpu.SemaphoreType.DMA((2,))],
        compiler_params=_params("parallel"),
        name="attn_sample",
    )(_later_key_matrix(SB_KEY_BLOCK), q, k_new, v_new, k_cache, v_cache, k_cache, v_cache)


def _out_proj_kernel(x_ref, attn_ref, conv_ref, ga_ref, w_ref, o_ref):
    sb = attn_ref.shape[1]
    attn_n = _rms(attn_ref[...], ga_ref[...]).astype(BF16)
    o_ref[...] = (x_ref[...]
                  + jnp.dot(attn_n, w_ref[:sb, :], preferred_element_type=F32)
                  + jnp.dot(conv_ref[...], w_ref[sb:, :], preferred_element_type=F32))


def _out_proj(x, attn, conv_n, g_attn, w, *, tm=512):
    m, d = x.shape
    sb, c = attn.shape[1], conv_n.shape[1]
    tm = min(tm, m)
    assert m % tm == 0 and w.shape == (sb + c, d)
    return pl.pallas_call(
        _out_proj_kernel,
        grid=(m // tm,),
        in_specs=[
            pl.BlockSpec((tm, d), lambda i: (i, 0)),
            pl.BlockSpec((tm, sb), lambda i: (i, 0)),
            pl.BlockSpec((tm, c), lambda i: (i, 0)),
            pl.BlockSpec((1, sb), lambda i: (0, 0)),
            pl.BlockSpec((sb + c, d), lambda i: (0, 0)),
        ],
        out_specs=pl.BlockSpec((tm, d), lambda i: (i, 0)),
        out_shape=jax.ShapeDtypeStruct((m, d), F32),
        compiler_params=_params("parallel"),
        name="out_proj",
    )(x, attn, conv_n, g_attn, w)


def _cast_kernel(w_ref, o_ref):
    o_ref[...] = w_ref[...].astype(o_ref.dtype)


def _layer_weight_bf16(w, layer, *, block_bytes=4 * 1024 * 1024):
    _, r, c = w.shape
    tr = r
    while tr * c * 4 > block_bytes and tr % 2 == 0 and (tr // 2) % 16 == 0:
        tr //= 2
    return pl.pallas_call(
        _cast_kernel,
        grid=(r // tr,),
        in_specs=[pl.BlockSpec((None, tr, c), lambda i: (layer, i, 0))],
        out_specs=pl.BlockSpec((tr, c), lambda i: (i, 0)),
        out_shape=jax.ShapeDtypeStruct((r, c), BF16),
        compiler_params=_params("parallel"),
        name="weight_bf16",
    )(w)


def _row(v):
    return v.reshape(1, -1).astype(F32)


def _layer(x, k_all, v_all, layer, cache, conv_hist, lw, g_final):
    b, t, d = x.shape
    x2 = x.reshape(b * t, d)
    x2 = _ffn(x2, lw["g_ffn1"], lw["w1_gate"], lw["w1_up"], lw["w1_down"])
    seq = lambda y: y.reshape(b, t, y.shape[1])
    conv_w = (lw["dw_w"], lw["dw_b"], lw["ln_g"], lw["ln_b"], lw["g_conv_out"])
    if cache is None:
        q, k_all, v_all, kb, vb, conv_n, new_hist = _in_proj_conv(
            x2, lw["g_mix"], lw["w_in"], k_all, v_all, layer, t, conv_hist, *conv_w)
        attn = _attn_prompt(seq(q), seq(kb), seq(vb))
    else:
        q, k_all, v_all, kb, vb, a = _in_proj(x2, lw["g_mix"], lw["w_in"], k_all, v_all, layer)
        attn = _attn_sample(seq(q), seq(kb), seq(vb), cache[0], cache[1], layer)
        conv_n = _conv(seq(a), conv_hist, *conv_w).reshape(b * t, -1)
        new_hist = seq(a)[:, t - HIST:, :]
    x2 = _out_proj(x2, attn.reshape(b * t, -1), conv_n, lw["g_attn_out"], lw["w_out"])
    x2 = _ffn(x2, lw["g_ffn2"], lw["w2_gate"], lw["w2_up"], lw["w2_down"], g_final)
    return x2.reshape(b, t, d), k_all, v_all, new_hist


def kernel(x_prompt, x_sample, cache_k, cache_v, state_conv, norm_ffn1, ffn1_gate, ffn1_up, ffn1_down, norm_mix, w_in, dw_weight, dw_bias, conv_ln_gain, conv_ln_bias, norm_attn_out, norm_conv_out, w_out, norm_ffn2, ffn2_gate, ffn2_up, ffn2_down, norm_final):
    depth = w_in.shape[0]
    conv_ch = dw_weight.shape[2]
    heads, hd = cache_k.shape[3], cache_k.shape[4]
    assert hd == SB_HEAD_DIM and x_prompt.shape[1] >= HIST and x_sample.shape[1] >= HIST
    (bp, sp, _), (bs, ts, _) = x_prompt.shape, x_sample.shape
    past = cache_k.shape[2]
    xp, xs = x_prompt, x_sample
    zero_hist = jnp.zeros((bp, HIST, conv_ch), F32)
    kp = jnp.zeros((depth, bp * sp * heads, hd), F32)
    vp = jnp.zeros((depth, bp * sp * heads, hd), F32)
    ks = jnp.zeros((depth, bs * ts * heads, hd), F32)
    vs = jnp.zeros((depth, bs * ts * heads, hd), F32)
    cache = (cache_k.reshape(depth, bs, past * heads, hd), cache_v.reshape(depth, bs, past * heads, hd))
    g_final = _row(norm_final)
    conv_p, conv_s = [], []
    for l in range(depth):
        lw = dict(
            g_ffn1=_row(norm_ffn1[l]), w1_gate=_layer_weight_bf16(ffn1_gate, l), w1_up=_layer_weight_bf16(ffn1_up, l),
            w1_down=_layer_weight_bf16(ffn1_down, l), g_mix=_row(norm_mix[l]), w_in=_layer_weight_bf16(w_in, l),
            dw_w=dw_weight[l].astype(F32), dw_b=_row(dw_bias[l]), ln_g=_row(conv_ln_gain[l]),
            ln_b=_row(conv_ln_bias[l]), g_attn_out=_row(norm_attn_out[l]), g_conv_out=_row(norm_conv_out[l]),
            w_out=_layer_weight_bf16(w_out, l), g_ffn2=_row(norm_ffn2[l]), w2_gate=_layer_weight_bf16(ffn2_gate, l),
            w2_up=_layer_weight_bf16(ffn2_up, l), w2_down=_layer_weight_bf16(ffn2_down, l))
        closing = g_final if l == depth - 1 else None
        xp, kp, vp, cp = _layer(xp, kp, vp, l, None, zero_hist, lw, closing)
        xs, ks, vs, cs = _layer(xs, ks, vs, l, cache, state_conv[l], lw, closing)
        conv_p.append(cp)
        conv_s.append(cs)
    return (xp, xs,
            kp.reshape(depth, bp, sp, heads, hd), vp.reshape(depth, bp, sp, heads, hd), jnp.stack(conv_p),
            ks.reshape(depth, bs, ts, heads, hd), vs.reshape(depth, bs, ts, heads, hd), jnp.stack(conv_s))
```

```python
import functools

import jax
import jax.numpy as jnp
from jax import lax
from jax.experimental import pallas as pl
from jax.experimental.pallas import tpu as pltpu

F32 = jnp.float32
BF16 = jnp.bfloat16

EPS = 1e-6
SB_HEAD_DIM = 128
SB_KEY_BLOCK = 128
CONV_WIDTH = 31
HIST = CONV_WIDTH - 1
SUBLANES = 8
HIST_PAD = 32

VMEM_LIMIT = 56 * 1024 * 1024

SB_DEAD_SUM = 106.0


def _params(*sem):
    return pltpu.CompilerParams(dimension_semantics=sem, vmem_limit_bytes=VMEM_LIMIT)


def _rms(x, g):
    return x * lax.rsqrt(jnp.mean(x * x, axis=-1, keepdims=True) + EPS) * g


def _ffn_kernel(x_ref, g_ref, wg_ref, wu_ref, wd_ref, gf_ref, o_ref, h_ref, *, final_norm):
    j = pl.program_id(1)

    @pl.when(j == 0)
    def _():
        x = x_ref[...]
        h_ref[...] = _rms(x, g_ref[...]).astype(BF16)
        o_ref[...] = x

    h = h_ref[...]
    gate = jnp.dot(h, wg_ref[...], preferred_element_type=F32)
    up = jnp.dot(h, wu_ref[...], preferred_element_type=F32)
    act = (0.5 * gate * jax.nn.sigmoid(gate) * up).astype(BF16)
    o_ref[...] += jnp.dot(act, wd_ref[...], preferred_element_type=F32)

    if final_norm:
        @pl.when(j == pl.num_programs(1) - 1)
        def _():
            o_ref[...] = _rms(o_ref[...], gf_ref[...])


def _ffn(x, g, wg, wu, wd, g_final=None, *, tm=1024, tf=512):
    m, d = x.shape
    f = wg.shape[1]
    tm = min(tm, m)
    assert m % tm == 0 and f % tf == 0
    vec = pl.BlockSpec((1, d), lambda i, j: (0, 0))
    return pl.pallas_call(
        functools.partial(_ffn_kernel, final_norm=g_final is not None),
        grid=(m // tm, f // tf),
        in_specs=[
            pl.BlockSpec((tm, d), lambda i, j: (i, 0)),
            vec,
            pl.BlockSpec((d, tf), lambda i, j: (0, j)),
            pl.BlockSpec((d, tf), lambda i, j: (0, j)),
            pl.BlockSpec((tf, d), lambda i, j: (j, 0)),
            vec,
        ],
        out_specs=pl.BlockSpec((tm, d), lambda i, j: (i, 0)),
        out_shape=jax.ShapeDtypeStruct((m, d), F32),
        scratch_shapes=[pltpu.VMEM((tm, d), BF16)],
        compiler_params=_params("parallel", "arbitrary"),
        name="ffn",
    )(x, g, wg, wu, wd, g if g_final is None else g_final)


def _conv_rows(src_ref, src0, sh_ref, dw_ref, db_ref, lg_ref, lb_ref, go_ref, o_ref, out0, *, sub, rows):
    n_shift = sub + HIST_PAD - SUBLANES
    for b in range(1, SUBLANES):
        sh_ref[b, 0:n_shift, :] = src_ref[src0 + b:src0 + b + n_shift, :]
    w0 = HIST_PAD - HIST
    groups = rows // SUBLANES
    for r0 in range(0, sub, rows):
        ys = [None] * groups
        for w in range(w0, w0 + CONV_WIDTH):
            tap = dw_ref[w - w0]
            for gi in range(groups):
                base = r0 + gi * SUBLANES + (w // SUBLANES) * SUBLANES
                if w % SUBLANES == 0:
                    frames = src_ref[src0 + base:src0 + base + SUBLANES, :]
                else:
                    frames = sh_ref[w % SUBLANES, base:base + SUBLANES, :]
                term = frames * tap
                ys[gi] = term if ys[gi] is None else ys[gi] + term
        y = jnp.concatenate(ys, axis=0) + db_ref[...]
        mu = jnp.mean(y, axis=-1, keepdims=True)
        yc = y - mu
        var = jnp.mean(yc * yc, axis=-1, keepdims=True)
        y = yc * lax.rsqrt(var + EPS) * lg_ref[...] + lb_ref[...]
        y = y * jax.nn.sigmoid(y)
        o_ref[out0 + r0:out0 + r0 + rows, :] = _rms(y, go_ref[...]).astype(o_ref.dtype)


def _start_history(a_ref, hist_ref):
    a_ref[0:SUBLANES, :] = jnp.zeros((SUBLANES, a_ref.shape[1]), F32)
    a_ref[HIST_PAD - HIST:HIST_PAD, :] = hist_ref[0]


def _conv_kernel(a_ref, prev_ref, hist_ref, dw_ref, db_ref, lg_ref, lb_ref, go_ref, o_ref,
                 full_ref, sh_ref, *, tt, rows):
    i = pl.program_id(1)

    @pl.when(i == 0)
    def _():
        _start_history(full_ref, hist_ref)

    @pl.when(i > 0)
    def _():
        full_ref[0:HIST_PAD, :] = prev_ref[0]

    full_ref[HIST_PAD:HIST_PAD + tt, :] = a_ref[0]
    _conv_rows(full_ref, 0, sh_ref, dw_ref, db_ref, lg_ref, lb_ref, go_ref, o_ref.at[0], 0, sub=tt, rows=rows)


def _conv_taps(dw_w):
    return jnp.broadcast_to(dw_w[:, None, :], (CONV_WIDTH, SUBLANES, dw_w.shape[1]))


def _conv(a, hist, dw_w, dw_b, ln_g, ln_b, g_out, *, tt=256, rows=32):
    b, t, c = a.shape
    tt = min(tt, t)
    assert t % tt == 0 and tt % rows == 0 and tt % HIST_PAD == 0 and rows % SUBLANES == 0
    per = tt // HIST_PAD
    vec = pl.BlockSpec((1, c), lambda bi, i: (0, 0))
    return pl.pallas_call(
        functools.partial(_conv_kernel, tt=tt, rows=rows),
        grid=(b, t // tt),
        in_specs=[
            pl.BlockSpec((1, tt, c), lambda bi, i: (bi, i, 0)),
            pl.BlockSpec((1, HIST_PAD, c), lambda bi, i: (bi, jnp.maximum(i * per - 1, 0), 0)),
            pl.BlockSpec((1, HIST, c), lambda bi, i: (bi, 0, 0)),
            pl.BlockSpec((CONV_WIDTH, SUBLANES, c), lambda bi, i: (0, 0, 0)),
            vec, vec, vec, vec,
        ],
        out_specs=pl.BlockSpec((1, tt, c), lambda bi, i: (bi, i, 0)),
        out_shape=jax.ShapeDtypeStruct((b, t, c), BF16),
        scratch_shapes=[pltpu.VMEM((HIST_PAD + tt, c), F32), pltpu.VMEM((SUBLANES, HIST_PAD + tt, c), F32)],
        compiler_params=_params("parallel", "arbitrary"),
        name="conv",
    )(a, a, hist, _conv_taps(dw_w), dw_b, ln_g, ln_b, g_out)


def _project_qkv(h, w_ref, q_ref, k_ref, v_ref, kb_ref, vb_ref, heads):
    tm = h.shape[0]
    wd = q_ref.shape[1]

    def group(j):
        return jnp.dot(h, w_ref[:, j * wd:(j + 1) * wd], preferred_element_type=F32)

    q_ref[...] = (group(0) * (SB_HEAD_DIM ** -0.5)).astype(BF16)
    for full_ref, half_ref, j in ((k_ref, kb_ref, 1), (v_ref, vb_ref, 2)):
        u = group(j)
        half_ref[...] = u.astype(BF16)
        for hh in range(heads):
            full_ref[pl.ds(hh, tm, stride=heads), :] = u[:, hh * SB_HEAD_DIM:(hh + 1) * SB_HEAD_DIM]


def _project_glu(h, w_ref, wd):
    value = jnp.dot(h, w_ref[:, 3 * wd:4 * wd], preferred_element_type=F32)
    gate = jnp.dot(h, w_ref[:, 4 * wd:5 * wd], preferred_element_type=F32)
    return value * jax.nn.sigmoid(gate)


def _in_proj_kernel(x_ref, g_ref, w_ref, k_all_ref, v_all_ref, q_ref, k_ref, v_ref, kb_ref, vb_ref, a_ref,
                    *, heads):
    del k_all_ref, v_all_ref
    h = _rms(x_ref[...], g_ref[...]).astype(BF16)
    a_ref[...] = _project_glu(h, w_ref, q_ref.shape[1])
    _project_qkv(h, w_ref, q_ref, k_ref, v_ref, kb_ref, vb_ref, heads)


def _in_proj_conv_kernel(x_ref, g_ref, w_ref, *refs, heads, tiles_per_seq, sub, rows, fresh_layer):
    if fresh_layer is None:
        refs = refs[2:]
    (hist_ref, dw_ref, db_ref, lg_ref, lb_ref, go_ref,
     q_ref, k_ref, v_ref, kb_ref, vb_ref, c_ref, hist_out_ref, a_ref, sh_ref) = refs
    if fresh_layer is not None:
        for stack_ref in (k_ref, v_ref):
            for l in range(stack_ref.shape[0]):
                if l != fresh_layer:
                    stack_ref[l] = jnp.zeros(stack_ref.shape[1:], F32)
        k_ref, v_ref = k_ref.at[fresh_layer], v_ref.at[fresh_layer]
    tm = x_ref.shape[0]
    first = pl.program_id(0) % tiles_per_seq == 0

    @pl.when(first)
    def _():
        _start_history(a_ref, hist_ref)

    @pl.when(jnp.logical_not(first))
    def _():
        a_ref[0:HIST_PAD, :] = a_ref[tm:tm + HIST_PAD, :]

    h = _rms(x_ref[...], g_ref[...]).astype(BF16)
    a = _project_glu(h, w_ref, q_ref.shape[1])
    a_ref[HIST_PAD:HIST_PAD + tm, :] = a
    hist_out_ref[0] = a[tm - HIST:, :]
    _project_qkv(h, w_ref, q_ref, k_ref, v_ref, kb_ref, vb_ref, heads)
    for s0 in range(0, tm, sub):
        _conv_rows(a_ref, s0, sh_ref, dw_ref, db_ref, lg_ref, lb_ref, go_ref, c_ref, s0, sub=sub, rows=rows)


def _in_proj_outs(m, wd, k_all, v_all, last_dtype):
    return [
        jax.ShapeDtypeStruct((m, wd), BF16),
        jax.ShapeDtypeStruct(k_all.shape, F32),
        jax.ShapeDtypeStruct(v_all.shape, F32),
        jax.ShapeDtypeStruct((m, wd), BF16),
        jax.ShapeDtypeStruct((m, wd), BF16),
        jax.ShapeDtypeStruct((m, wd), last_dtype),
    ]


def _in_proj(x, g, w, k_all, v_all, layer, *, tm=512):
    m, d = x.shape
    wd = w.shape[1] // 5
    heads = wd // SB_HEAD_DIM
    tm = min(tm, m)
    assert m % tm == 0 and w.shape[1] == 5 * wd and k_all.shape[1:] == (m * heads, SB_HEAD_DIM)
    col = pl.BlockSpec((tm, wd), lambda i: (i, 0))
    rows = pl.BlockSpec((None, tm * heads, SB_HEAD_DIM), lambda i: (layer, i, 0))
    anyspec = pl.BlockSpec(memory_space=pl.ANY)
    return pl.pallas_call(
        functools.partial(_in_proj_kernel, heads=heads),
        grid=(m // tm,),
        in_specs=[
            pl.BlockSpec((tm, d), lambda i: (i, 0)),
            pl.BlockSpec((1, d), lambda i: (0, 0)),
            pl.BlockSpec((d, 5 * wd), lambda i: (0, 0)),
            anyspec, anyspec,
        ],
        out_specs=[col, rows, rows, col, col, col],
        out_shape=_in_proj_outs(m, wd, k_all, v_all, F32),
        input_output_aliases={3: 1, 4: 2},
        compiler_params=_params("parallel"),
        name="in_proj",
    )(x, g, w, k_all, v_all)


def _in_proj_conv(x, g, w, k_all, v_all, layer, depth, seq_len, hist, dw_w, dw_b, ln_g, ln_b, g_out,
                  *, tm=256, sub=128, rows=32):
    m, d = x.shape
    wd = w.shape[1] // 5
    heads = wd // SB_HEAD_DIM
    n_seq = m // seq_len
    assert seq_len % tm == 0 and tm % sub == 0 and sub % rows == 0 and rows % SUBLANES == 0 and tm >= HIST_PAD
    fresh = k_all is None
    stack = jax.ShapeDtypeStruct((depth, m * heads, SB_HEAD_DIM), F32)
    assert w.shape[1] == 5 * wd and hist.shape == (n_seq, HIST, wd)
    assert fresh or (k_all.shape == stack.shape and v_all.shape == stack.shape)
    tiles_per_seq = seq_len // tm
    col = pl.BlockSpec((tm, wd), lambda i: (i, 0))
    if fresh:
        krows = pl.BlockSpec((depth, tm * heads, SB_HEAD_DIM), lambda i: (0, i, 0))
    else:
        krows = pl.BlockSpec((None, tm * heads, SB_HEAD_DIM), lambda i: (layer, i, 0))
    hist_spec = pl.BlockSpec((1, HIST, wd), lambda i: (i // tiles_per_seq, 0, 0))
    vec = pl.BlockSpec((1, wd), lambda i: (0, 0))
    anyspec = pl.BlockSpec(memory_space=pl.ANY)
    return pl.pallas_call(
        functools.partial(_in_proj_conv_kernel, heads=heads, tiles_per_seq=tiles_per_seq, sub=sub, rows=rows,
                          fresh_layer=layer if fresh else None),
        grid=(m // tm,),
        in_specs=[
            pl.BlockSpec((tm, d), lambda i: (i, 0)),
            pl.BlockSpec((1, d), lambda i: (0, 0)),
            pl.BlockSpec((d, 5 * wd), lambda i: (0, 0)),
            *([] if fresh else [anyspec, anyspec]),
            hist_spec,
            pl.BlockSpec((CONV_WIDTH, SUBLANES, wd), lambda i: (0, 0, 0)),
            vec, vec, vec, vec,
        ],
        out_specs=[col, krows, krows, col, col, col, hist_spec],
        out_shape=_in_proj_outs(m, wd, stack, stack, BF16) + [jax.ShapeDtypeStruct((n_seq, HIST, wd), F32)],
        input_output_aliases={} if fresh else {3: 1, 4: 2},
        scratch_shapes=[pltpu.VMEM((HIST_PAD + tm, wd), F32), pltpu.VMEM((SUBLANES, HIST_PAD + sub, wd), F32)],
        compiler_params=_params("arbitrary"),
        name="in_proj_conv",
    )(x, g, w, *([] if fresh else [k_all, v_all]), hist, _conv_taps(dw_w), dw_b, ln_g, ln_b, g_out)


def _later_key_matrix(tk):
    j = lax.broadcasted_iota(jnp.int32, (tk, tk), 0)
    s = lax.broadcasted_iota(jnp.int32, (tk, tk), 1)
    return (j > s).astype(BF16)


def _strictly_earlier(tq, tk):
    t = lax.broadcasted_iota(jnp.int32, (tq, tk), 0)
    s = lax.broadcasted_iota(jnp.int32, (tq, tk), 1)
    return s < t


def _sb_chains(chains):
    def head_rows(x, top):
        return x if top is None else x[:top]

    def add_to_head_rows(x, part, top):
        return x + part if top is None else jnp.concatenate([x[:top] + part, x[top:]], axis=0)

    logits = [[lax.dot_general(head_rows(q, top), kb, (((1,), (1,)), ((), ())), preferred_element_type=F32)
               for kb, _, _, _, top in blocks] for q, _, blocks in chains]
    staged, carries = [], []
    for (_, carry, blocks), zs in zip(chains, logits):
        row = []
        for (_, _, mask, _, top), z in zip(blocks, zs):
            l = jnp.log(1.0 + jnp.exp(-jnp.abs(z)))
            sp = jnp.maximum(z, 0.0) + l
            log_beta = jnp.minimum(z, 0.0) - l
            if mask is not None:
                sp = jnp.where(mask, sp, 0.0)
            hi = sp.astype(BF16)
            split = jnp.concatenate([hi, (sp - hi.astype(F32)).astype(BF16)], axis=-1)
            row.append((log_beta, split, head_rows(carry, top)))
            carry = add_to_head_rows(carry, jnp.sum(sp, axis=-1, keepdims=True), top)
        staged.append(row)
        carries.append(carry)
    between = [[jnp.dot(split, jnp.concatenate([blk[3], blk[3]], axis=0), preferred_element_type=F32)
                for (_, split, _), blk in zip(row, blocks)]
               for row, (_, _, blocks) in zip(staged, chains)]
    weights = []
    for row, bts, (_, _, blocks) in zip(staged, between, chains):
        ws = []
        for (log_beta, _, carry), bt, blk in zip(row, bts, blocks):
            w = jnp.exp(log_beta - bt - carry)
            if blk[2] is not None:
                w = jnp.where(blk[2], w, 0.0)
            ws.append(w.astype(BF16))
        weights.append(ws)
    outs = []
    for ws, (_, _, blocks) in zip(weights, chains):
        out = None
        for w, (_, vb, _, _, top) in zip(ws, blocks):
            pv = jnp.dot(w, vb, preferred_element_type=F32)
            out = pv if out is None else add_to_head_rows(out, pv, top)
        outs.append(out)
    return list(zip(outs, carries))


def _attn_prompt_kernel(later_ref, q_ref, k_ref, v_ref, o_ref, carry_ref, *, heads, halves, nprev, top):
    tb = SB_KEY_BLOCK
    i = pl.program_id(2)
    head_lanes = [slice(hh * SB_HEAD_DIM, (hh + 1) * SB_HEAD_DIM) for hh in range(heads)]
    q_rows = [slice(sub * tb, (sub + 1) * tb) for sub in range(halves)]
    top_rows = [slice(sub * tb, sub * tb + top) for sub in range(halves)]
    tail_rows = [slice(sub * tb + top, (sub + 1) * tb) for sub in range(halves)]
    diag = _strictly_earlier(tb, tb)

    def kv(blk, lanes):
        rows = pl.ds(pl.multiple_of(blk * tb, tb), tb)
        return k_ref[0, rows, lanes], v_ref[0, rows, lanes]

    def unconditional(first_step):
        later = later_ref[...]
        where, chains = [], []
        for hh, lanes in enumerate(head_lanes):
            for sub in range(halves):
                n_back = min(sub, nprev) if first_step else nprev
                blocks = [kv(i * halves + sub - back, lanes)
                          + (diag if back == 0 else None, later, top if back == nprev else None)
                          for back in range(n_back + 1)]
                chains.append((q_ref[0, q_rows[sub], lanes], jnp.zeros((tb, 1), F32), blocks))
                where.append((hh, sub, lanes))
        for (hh, sub, lanes), (out, carry) in zip(where, _sb_chains(chains)):
            o_ref[0, q_rows[sub], lanes] = out
            carry_ref[hh, q_rows[sub], :] = carry

    @pl.when(i == 0)
    def _():
        unconditional(True)

    @pl.when(i > 0)
    def _():
        unconditional(False)

    def cond(state):
        n, low = state
        newest = i * halves + (halves - 1) - nprev - n
        return jnp.logical_and(newest >= 0, low < SB_DEAD_SUM)

    def sweep(rows, blk):
        chains = [(q_ref[0, rows, lanes], carry_ref[hh, rows, :], [kv(blk, lanes) + (None, later_ref[...], None)])
                  for hh, lanes in enumerate(head_lanes)]
        for (hh, lanes), (out, carry) in zip(enumerate(head_lanes), _sb_chains(chains)):
            o_ref[0, rows, lanes] += out
            carry_ref[hh, rows, :] = carry

    def body(state):
        n, _ = state
        for sub in range(halves):
            blk = i * halves + sub - nprev - n

            @pl.when(blk >= 0)
            def _():
                sweep(tail_rows[sub], blk)

            @pl.when(blk >= 1)
            def _():
                sweep(top_rows[sub], blk - 1)
        return n + 1, jnp.min(carry_ref[...])

    lax.while_loop(cond, body, (jnp.int32(0), jnp.min(carry_ref[...])))


def _attn_prompt(q, k, v, *, heads=4, halves=4, nprev=2, top=48):
    b, s, w = q.shape
    heads = min(heads, w // SB_HEAD_DIM)
    hw = heads * SB_HEAD_DIM
    tq = halves * SB_KEY_BLOCK
    assert s % tq == 0 and w % hw == 0 and nprev >= 1 and 0 < top < SB_KEY_BLOCK and top % 16 == 0
    kv_spec = pl.BlockSpec((1, s, hw), lambda bi, g, i: (bi, 0, g))
    q_spec = pl.BlockSpec((1, tq, hw), lambda bi, g, i: (bi, i, g))
    return pl.pallas_call(
        functools.partial(_attn_prompt_kernel, heads=heads, halves=halves, nprev=nprev, top=top),
        grid=(b, w // hw, s // tq),
        in_specs=[pl.BlockSpec((SB_KEY_BLOCK, SB_KEY_BLOCK), lambda bi, g, i: (0, 0)), q_spec, kv_spec, kv_spec],
        out_specs=q_spec,
        out_shape=jax.ShapeDtypeStruct((b, s, w), F32),
        scratch_shapes=[pltpu.VMEM((heads, tq, 1), F32)],
        compiler_params=_params("parallel", "parallel", "arbitrary"),
        name="attn_prompt",
    )(_later_key_matrix(SB_KEY_BLOCK), q, k, v)


def _attn_sample_kernel(later_ref, q_ref, kn_ref, vn_ref, kw_ref, vw_ref, kc_hbm, vc_hbm, o_ref,
                        carry_ref, kbuf, vbuf, sem, *, layer, heads, window):
    tb = SB_KEY_BLOCK
    b = pl.program_id(0)
    t = q_ref.shape[1]
    n_past = kc_hbm.shape[2] // (heads * tb)
    head_lanes = [slice(hh * SB_HEAD_DIM, (hh + 1) * SB_HEAD_DIM) for hh in range(heads)]

    def head_kv(k_rows_ref, v_rows_ref, blk, hh):
        rows = pl.ds(blk * tb * heads + hh, tb, stride=heads)
        return k_rows_ref[rows, :].astype(BF16), v_rows_ref[rows, :].astype(BF16)

    later = later_ref[...]
    chains = []
    for hh, lanes in enumerate(head_lanes):
        blocks = [(kn_ref[0, :, lanes], vn_ref[0, :, lanes], _strictly_earlier(t, t), later_ref[:t, :t], None)]
        blocks += [head_kv(kw_ref, vw_ref, blk, hh) + (None, later, None) for blk in range(window - 1, -1, -1)]
        chains.append((q_ref[0, :, lanes], jnp.zeros((t, 1), F32), blocks))
    for (hh, lanes), (out, carry) in zip(enumerate(head_lanes), _sb_chains(chains)):
        o_ref[0, :, lanes] = out
        carry_ref[hh] = carry

    def cond(state):
        j, low = state
        return jnp.logical_and(j >= 0, low < SB_DEAD_SUM)

    def body(state):
        j, _ = state
        rows = pl.ds(pl.multiple_of(j * (tb * heads), tb * heads), tb * heads)
        copies = [pltpu.make_async_copy(src.at[layer, b, rows, :], dst, sem.at[n])
                  for n, (src, dst) in enumerate(((kc_hbm, kbuf), (vc_hbm, vbuf)))]
        for cp in copies:
            cp.start()
        for cp in copies:
            cp.wait()
        chains = [(q_ref[0, :, lanes], carry_ref[hh], [head_kv(kbuf, vbuf, 0, hh) + (None, later_ref[...], None)])
                  for hh, lanes in enumerate(head_lanes)]
        for (hh, lanes), (out, carry) in zip(enumerate(head_lanes), _sb_chains(chains)):
            o_ref[0, :, lanes] += out
            carry_ref[hh] = carry
        return j - 1, jnp.min(carry_ref[...])

    lax.while_loop(cond, body, (jnp.int32(n_past - window - 1), jnp.min(carry_ref[...])))


def _attn_sample(q, k_new, v_new, k_cache, v_cache, layer, *, window=2):
    b, t, w = q.shape
    heads = w // SB_HEAD_DIM
    blk_rows = SB_KEY_BLOCK * heads
    n_past = k_cache.shape[2] // blk_rows
    assert k_cache.shape[2] == n_past * blk_rows and n_past % window == 0 and t <= SB_KEY_BLOCK
    new_spec = pl.BlockSpec((1, t, w), lambda bi: (bi, 0, 0))
    win_spec = pl.BlockSpec((None, None, window * blk_rows, SB_HEAD_DIM),
                            lambda bi: (layer, bi, n_past // window - 1, 0))
    anyspec = pl.BlockSpec(memory_space=pl.ANY)
    return pl.pallas_call(
        functools.partial(_attn_sample_kernel, layer=layer, heads=heads, window=window),
        grid=(b,),
        in_specs=[pl.BlockSpec((SB_KEY_BLOCK, SB_KEY_BLOCK), lambda bi: (0, 0)),
                  new_spec, new_spec, new_spec, win_spec, win_spec, anyspec, anyspec],
        out_specs=new_spec,
        out_shape=jax.ShapeDtypeStruct((b, t, w), F32),
        scratch_shapes=[pltpu.VMEM((heads, t, 1), F32),
                        pltpu.VMEM((blk_rows, SB_HEAD_DIM), F32), pltpu.VMEM((blk_rows, SB_HEAD_DIM), F32),
                        pltpu.SemaphoreType.DMA((2,))],
        compiler_params=_params("parallel"),
        name="attn_sample",
    )(_later_key_matrix(SB_KEY_BLOCK), q, k_new, v_new, k_cache, v_cache, k_cache, v_cache)


def _out_proj_kernel(x_ref, attn_ref, conv_ref, ga_ref, w_ref, o_ref):
    sb = attn_ref.shape[1]
    attn_n = _rms(attn_ref[...], ga_ref[...]).astype(BF16)
    o_ref[...] = (x_ref[...]
                  + jnp.dot(attn_n, w_ref[:sb, :], preferred_element_type=F32)
                  + jnp.dot(conv_ref[...], w_ref[sb:, :], preferred_element_type=F32))


def _out_proj(x, attn, conv_n, g_attn, w, *, tm=512):
    m, d = x.shape
    sb, c = attn.shape[1], conv_n.shape[1]
    tm = min(tm, m)
    assert m % tm == 0 and w.shape == (sb + c, d)
    return pl.pallas_call(
        _out_proj_kernel,
        grid=(m // tm,),
        in_specs=[
            pl.BlockSpec((tm, d), lambda i: (i, 0)),
            pl.BlockSpec((tm, sb), lambda i: (i, 0)),
            pl.BlockSpec((tm, c), lambda i: (i, 0)),
            pl.BlockSpec((1, sb), lambda i: (0, 0)),
            pl.BlockSpec((sb + c, d), lambda i: (0, 0)),
        ],
        out_specs=pl.BlockSpec((tm, d), lambda i: (i, 0)),
        out_shape=jax.ShapeDtypeStruct((m, d), F32),
        compiler_params=_params("parallel"),
        name="out_proj",
    )(x, attn, conv_n, g_attn, w)


def _cast_kernel(w_ref, o_ref):
    o_ref[...] = w_ref[...].astype(o_ref.dtype)


def _layer_weight_bf16(w, layer, *, block_bytes=4 * 1024 * 1024):
    _, r, c = w.shape
    tr = r
    while tr * c * 4 > block_bytes and tr % 2 == 0 and (tr // 2) % 16 == 0:
        tr //= 2
    return pl.pallas_call(
        _cast_kernel,
        grid=(r // tr,),
        in_specs=[pl.BlockSpec((None, tr, c), lambda i: (layer, i, 0))],
        out_specs=pl.BlockSpec((tr, c), lambda i: (i, 0)),
        out_shape=jax.ShapeDtypeStruct((r, c), BF16),
        compiler_params=_params("parallel"),
        name="weight_bf16",
    )(w)


def _row(v):
    return v.reshape(1, -1).astype(F32)


def _layer(x, k_all, v_all, layer, depth, cache, conv_hist, lw, g_final):
    b, t, d = x.shape
    x2 = x.reshape(b * t, d)
    x2 = _ffn(x2, lw["g_ffn1"], lw["w1_gate"], lw["w1_up"], lw["w1_down"])
    seq = lambda y: y.reshape(b, t, y.shape[1])
    conv_w = (lw["dw_w"], lw["dw_b"], lw["ln_g"], lw["ln_b"], lw["g_conv_out"])
    if cache is None:
        q, k_all, v_all, kb, vb, conv_n, new_hist = _in_proj_conv(
            x2, lw["g_mix"], lw["w_in"], k_all, v_all, layer, depth, t, conv_hist, *conv_w)
        attn = _attn_prompt(seq(q), seq(kb), seq(vb))
    else:
        q, k_all, v_all, kb, vb, a = _in_proj(x2, lw["g_mix"], lw["w_in"], k_all, v_all, layer)
        attn = _attn_sample(seq(q), seq(kb), seq(vb), cache[0], cache[1], layer)
        conv_n = _conv(seq(a), conv_hist, *conv_w).reshape(b * t, -1)
        new_hist = seq(a)[:, t - HIST:, :]
    x2 = _out_proj(x2, attn.reshape(b * t, -1), conv_n, lw["g_attn_out"], lw["w_out"])
    x2 = _ffn(x2, lw["g_ffn2"], lw["w2_gate"], lw["w2_up"], lw["w2_down"], g_final)
    return x2.reshape(b, t, d), k_all, v_all, new_hist


def kernel(x_prompt, x_sample, cache_k, cache_v, state_conv, norm_ffn1, ffn1_gate, ffn1_up, ffn1_down, norm_mix, w_in, dw_weight, dw_bias, conv_ln_gain, conv_ln_bias, norm_attn_out, norm_conv_out, w_out, norm_ffn2, ffn2_gate, ffn2_up, ffn2_down, norm_final):
    depth = w_in.shape[0]
    conv_ch = dw_weight.shape[2]
    heads, hd = cache_k.shape[3], cache_k.shape[4]
    assert hd == SB_HEAD_DIM and x_prompt.shape[1] >= HIST and x_sample.shape[1] >= HIST
    (bp, sp, _), (bs, ts, _) = x_prompt.shape, x_sample.shape
    past = cache_k.shape[2]
    xp, xs = x_prompt, x_sample
    zero_hist = jnp.zeros((bp, HIST, conv_ch), F32)
    kp = vp = None
    ks = jnp.zeros((depth, bs * ts * heads, hd), F32)
    vs = jnp.zeros((depth, bs * ts * heads, hd), F32)
    cache = (cache_k.reshape(depth, bs, past * heads, hd), cache_v.reshape(depth, bs, past * heads, hd))
    g_final = _row(norm_final)
    conv_p, conv_s = [], []
    for l in range(depth):
        lw = dict(
            g_ffn1=_row(norm_ffn1[l]), w1_gate=_layer_weight_bf16(ffn1_gate, l), w1_up=_layer_weight_bf16(ffn1_up, l),
            w1_down=_layer_weight_bf16(ffn1_down, l), g_mix=_row(norm_mix[l]), w_in=_layer_weight_bf16(w_in, l),
            dw_w=dw_weight[l].astype(F32), dw_b=_row(dw_bias[l]), ln_g=_row(conv_ln_gain[l]),
            ln_b=_row(conv_ln_bias[l]), g_attn_out=_row(norm_attn_out[l]), g_conv_out=_row(norm_conv_out[l]),
            w_out=_layer_weight_bf16(w_out, l), g_ffn2=_row(norm_ffn2[l]), w2_gate=_layer_weight_bf16(ffn2_gate, l),
            w2_up=_layer_weight_bf16(ffn2_up, l), w2_down=_layer_weight_bf16(ffn2_down, l))
        closing = g_final if l == depth - 1 else None
        xp, kp, vp, cp = _layer(xp, kp, vp, l, depth, None, zero_hist, lw, closing)
        xs, ks, vs, cs = _layer(xs, ks, vs, l, depth, cache, state_conv[l], lw, closing)
        conv_p.append(cp)
        conv_s.append(cs)
    return (xp, xs,
            kp.reshape(depth, bp, sp, heads, hd), vp.reshape(depth, bp, sp, heads, hd), jnp.stack(conv_p),
            ks.reshape(depth, bs, ts, heads, hd), vs.reshape(depth, bs, ts, heads, hd), jnp.stack(conv_s))
```

```python
import functools

import jax
import jax.numpy as jnp
from jax import lax
from jax.experimental import pallas as pl
from jax.experimental.pallas import tpu as pltpu

F32 = jnp.float32
BF16 = jnp.bfloat16

EPS = 1e-6
SB_HEAD_DIM = 128
SB_KEY_BLOCK = 128
CONV_WIDTH = 31
HIST = CONV_WIDTH - 1
SUBLANES = 8
HIST_PAD = 32
FFN_CHUNK = 512

VMEM_LIMIT = 56 * 1024 * 1024

SB_DEAD_SUM = 106.0


def _params(*sem):
    return pltpu.CompilerParams(dimension_semantics=sem, vmem_limit_bytes=VMEM_LIMIT)


def _rms(x, g):
    return x * lax.rsqrt(jnp.mean(x * x, axis=-1, keepdims=True) + EPS) * g


def _ffn_kernel(x_ref, g_ref, wgu_ref, wd_ref, gf_ref, o_ref, h_ref, *, final_norm):
    j = pl.program_id(1)
    tf = wd_ref.shape[0]

    @pl.when(j == 0)
    def _():
        x = x_ref[...]
        h_ref[...] = _rms(x, g_ref[...]).astype(BF16)
        o_ref[...] = x

    gate_up = jnp.dot(h_ref[...], wgu_ref[...], preferred_element_type=F32)
    gate, up = gate_up[:, :tf], gate_up[:, tf:]
    act = (0.5 * gate * jax.nn.sigmoid(gate) * up).astype(BF16)
    o_ref[...] += jnp.dot(act, wd_ref[...], preferred_element_type=F32)

    if final_norm:
        @pl.when(j == pl.num_programs(1) - 1)
        def _():
            o_ref[...] = _rms(o_ref[...], gf_ref[...])


def _ffn(x, g, wgu, wd, g_final=None, *, tm=1024):
    m, d = x.shape
    n_chunks, _, two_tf = wgu.shape
    tf = two_tf // 2
    tm = min(tm, m)
    assert m % tm == 0 and wd.shape == (n_chunks * tf, d)
    vec = pl.BlockSpec((1, d), lambda i, j: (0, 0))
    return pl.pallas_call(
        functools.partial(_ffn_kernel, final_norm=g_final is not None),
        grid=(m // tm, n_chunks),
        in_specs=[
            pl.BlockSpec((tm, d), lambda i, j: (i, 0)),
            vec,
            pl.BlockSpec((None, d, two_tf), lambda i, j: (j, 0, 0)),
            pl.BlockSpec((tf, d), lambda i, j: (j, 0)),
            vec,
        ],
        out_specs=pl.BlockSpec((tm, d), lambda i, j: (i, 0)),
        out_shape=jax.ShapeDtypeStruct((m, d), F32),
        scratch_shapes=[pltpu.VMEM((tm, d), BF16)],
        compiler_params=_params("parallel", "arbitrary"),
        name="ffn",
    )(x, g, wgu, wd, g if g_final is None else g_final)


def _conv_rows(src_ref, src0, sh_ref, dw_ref, db_ref, lg_ref, lb_ref, go_ref, o_ref, out0, *, sub, rows):
    n_shift = sub + HIST_PAD - SUBLANES
    for b in range(1, SUBLANES):
        sh_ref[b, 0:n_shift, :] = src_ref[src0 + b:src0 + b + n_shift, :]
    w0 = HIST_PAD - HIST
    groups = rows // SUBLANES
    for r0 in range(0, sub, rows):
        ys = [None] * groups
        for w in range(w0, w0 + CONV_WIDTH):
            tap = dw_ref[w - w0]
            for gi in range(groups):
                base = r0 + gi * SUBLANES + (w // SUBLANES) * SUBLANES
                if w % SUBLANES == 0:
                    frames = src_ref[src0 + base:src0 + base + SUBLANES, :]
                else:
                    frames = sh_ref[w % SUBLANES, base:base + SUBLANES, :]
                term = frames * tap
                ys[gi] = term if ys[gi] is None else ys[gi] + term
        y = jnp.concatenate(ys, axis=0) + db_ref[...]
        mu = jnp.mean(y, axis=-1, keepdims=True)
        yc = y - mu
        var = jnp.mean(yc * yc, axis=-1, keepdims=True)
        y = yc * lax.rsqrt(var + EPS) * lg_ref[...] + lb_ref[...]
        y = y * jax.nn.sigmoid(y)
        o_ref[out0 + r0:out0 + r0 + rows, :] = _rms(y, go_ref[...]).astype(o_ref.dtype)


def _start_history(a_ref, hist_ref):
    a_ref[0:SUBLANES, :] = jnp.zeros((SUBLANES, a_ref.shape[1]), F32)
    a_ref[HIST_PAD - HIST:HIST_PAD, :] = hist_ref[0]


def _conv_kernel(a_ref, prev_ref, hist_ref, dw_ref, db_ref, lg_ref, lb_ref, go_ref, o_ref,
                 full_ref, sh_ref, *, tt, rows):
    i = pl.program_id(1)

    @pl.when(i == 0)
    def _():
        _start_history(full_ref, hist_ref)

    @pl.when(i > 0)
    def _():
        full_ref[0:HIST_PAD, :] = prev_ref[0]

    full_ref[HIST_PAD:HIST_PAD + tt, :] = a_ref[0]
    _conv_rows(full_ref, 0, sh_ref, dw_ref, db_ref, lg_ref, lb_ref, go_ref, o_ref.at[0], 0, sub=tt, rows=rows)


def _conv_taps(dw_w):
    return jnp.broadcast_to(dw_w[:, None, :], (CONV_WIDTH, SUBLANES, dw_w.shape[1]))


def _conv(a, hist, dw_w, dw_b, ln_g, ln_b, g_out, *, tt=256, rows=32):
    b, t, c = a.shape
    tt = min(tt, t)
    assert t % tt == 0 and tt % rows == 0 and tt % HIST_PAD == 0 and rows % SUBLANES == 0
    per = tt // HIST_PAD
    vec = pl.BlockSpec((1, c), lambda bi, i: (0, 0))
    return pl.pallas_call(
        functools.partial(_conv_kernel, tt=tt, rows=rows),
        grid=(b, t // tt),
        in_specs=[
            pl.BlockSpec((1, tt, c), lambda bi, i: (bi, i, 0)),
            pl.BlockSpec((1, HIST_PAD, c), lambda bi, i: (bi, jnp.maximum(i * per - 1, 0), 0)),
            pl.BlockSpec((1, HIST, c), lambda bi, i: (bi, 0, 0)),
            pl.BlockSpec((CONV_WIDTH, SUBLANES, c), lambda bi, i: (0, 0, 0)),
            vec, vec, vec, vec,
        ],
        out_specs=pl.BlockSpec((1, tt, c), lambda bi, i: (bi, i, 0)),
        out_shape=jax.ShapeDtypeStruct((b, t, c), BF16),
        scratch_shapes=[pltpu.VMEM((HIST_PAD + tt, c), F32), pltpu.VMEM((SUBLANES, HIST_PAD + tt, c), F32)],
        compiler_params=_params("parallel", "arbitrary"),
        name="conv",
    )(a, a, hist, _conv_taps(dw_w), dw_b, ln_g, ln_b, g_out)


def _project_qkv(h, w_ref, q_ref, k_ref, v_ref, kb_ref, vb_ref, heads):
    tm = h.shape[0]
    wd = q_ref.shape[1]

    def group(j):
        return jnp.dot(h, w_ref[:, j * wd:(j + 1) * wd], preferred_element_type=F32)

    q_ref[...] = (group(0) * (SB_HEAD_DIM ** -0.5)).astype(BF16)
    for full_ref, half_ref, j in ((k_ref, kb_ref, 1), (v_ref, vb_ref, 2)):
        u = group(j)
        half_ref[...] = u.astype(BF16)
        for hh in range(heads):
            full_ref[pl.ds(hh, tm, stride=heads), :] = u[:, hh * SB_HEAD_DIM:(hh + 1) * SB_HEAD_DIM]


def _project_glu(h, w_ref, wd):
    value = jnp.dot(h, w_ref[:, 3 * wd:4 * wd], preferred_element_type=F32)
    gate = jnp.dot(h, w_ref[:, 4 * wd:5 * wd], preferred_element_type=F32)
    return value * jax.nn.sigmoid(gate)


def _in_proj_kernel(x_ref, g_ref, w_ref, k_all_ref, v_all_ref, q_ref, k_ref, v_ref, kb_ref, vb_ref, a_ref,
                    *, heads):
    del k_all_ref, v_all_ref
    h = _rms(x_ref[...], g_ref[...]).astype(BF16)
    a_ref[...] = _project_glu(h, w_ref, q_ref.shape[1])
    _project_qkv(h, w_ref, q_ref, k_ref, v_ref, kb_ref, vb_ref, heads)


def _in_proj_conv_kernel(x_ref, g_ref, w_ref, *refs, heads, tiles_per_seq, sub, rows, fresh_layer):
    if fresh_layer is None:
        refs = refs[2:]
    (hist_ref, dw_ref, db_ref, lg_ref, lb_ref, go_ref,
     q_ref, k_ref, v_ref, kb_ref, vb_ref, c_ref, hist_out_ref, a_ref, sh_ref) = refs
    if fresh_layer is not None:
        for stack_ref in (k_ref, v_ref):
            for l in range(stack_ref.shape[0]):
                if l != fresh_layer:
                    stack_ref[l] = jnp.zeros(stack_ref.shape[1:], F32)
        k_ref, v_ref = k_ref.at[fresh_layer], v_ref.at[fresh_layer]
    tm = x_ref.shape[0]
    first = pl.program_id(0) % tiles_per_seq == 0

    @pl.when(first)
    def _():
        _start_history(a_ref, hist_ref)

    @pl.when(jnp.logical_not(first))
    def _():
        a_ref[0:HIST_PAD, :] = a_ref[tm:tm + HIST_PAD, :]

    h = _rms(x_ref[...], g_ref[...]).astype(BF16)
    a = _project_glu(h, w_ref, q_ref.shape[1])
    a_ref[HIST_PAD:HIST_PAD + tm, :] = a
    hist_out_ref[0] = a[tm - HIST:, :]
    _project_qkv(h, w_ref, q_ref, k_ref, v_ref, kb_ref, vb_ref, heads)
    for s0 in range(0, tm, sub):
        _conv_rows(a_ref, s0, sh_ref, dw_ref, db_ref, lg_ref, lb_ref, go_ref, c_ref, s0, sub=sub, rows=rows)


def _in_proj_outs(m, wd, k_all, v_all, last_dtype):
    return [
        jax.ShapeDtypeStruct((m, wd), BF16),
        jax.ShapeDtypeStruct(k_all.shape, F32),
        jax.ShapeDtypeStruct(v_all.shape, F32),
        jax.ShapeDtypeStruct((m, wd), BF16),
        jax.ShapeDtypeStruct((m, wd), BF16),
        jax.ShapeDtypeStruct((m, wd), last_dtype),
    ]


def _in_proj(x, g, w, k_all, v_all, layer, *, tm=512):
    m, d = x.shape
    wd = w.shape[1] // 5
    heads = wd // SB_HEAD_DIM
    tm = min(tm, m)
    assert m % tm == 0 and w.shape[1] == 5 * wd and k_all.shape[1:] == (m * heads, SB_HEAD_DIM)
    col = pl.BlockSpec((tm, wd), lambda i: (i, 0))
    rows = pl.BlockSpec((None, tm * heads, SB_HEAD_DIM), lambda i: (layer, i, 0))
    anyspec = pl.BlockSpec(memory_space=pl.ANY)
    return pl.pallas_call(
        functools.partial(_in_proj_kernel, heads=heads),
        grid=(m // tm,),
        in_specs=[
            pl.BlockSpec((tm, d), lambda i: (i, 0)),
            pl.BlockSpec((1, d), lambda i: (0, 0)),
            pl.BlockSpec((d, 5 * wd), lambda i: (0, 0)),
            anyspec, anyspec,
        ],
        out_specs=[col, rows, rows, col, col, col],
        out_shape=_in_proj_outs(m, wd, k_all, v_all, F32),
        input_output_aliases={3: 1, 4: 2},
        compiler_params=_params("parallel"),
        name="in_proj",
    )(x, g, w, k_all, v_all)


def _in_proj_conv(x, g, w, k_all, v_all, layer, depth, seq_len, hist, dw_w, dw_b, ln_g, ln_b, g_out,
                  *, tm=256, sub=128, rows=32):
    m, d = x.shape
    wd = w.shape[1] // 5
    heads = wd // SB_HEAD_DIM
    n_seq = m // seq_len
    assert seq_len % tm == 0 and tm % sub == 0 and sub % rows == 0 and rows % SUBLANES == 0 and tm >= HIST_PAD
    fresh = k_all is None
    stack = jax.ShapeDtypeStruct((depth, m * heads, SB_HEAD_DIM), F32)
    assert w.shape[1] == 5 * wd and hist.shape == (n_seq, HIST, wd)
    assert fresh or (k_all.shape == stack.shape and v_all.shape == stack.shape)
    tiles_per_seq = seq_len // tm
    col = pl.BlockSpec((tm, wd), lambda i: (i, 0))
    if fresh:
        krows = pl.BlockSpec((depth, tm * heads, SB_HEAD_DIM), lambda i: (0, i, 0))
    else:
        krows = pl.BlockSpec((None, tm * heads, SB_HEAD_DIM), lambda i: (layer, i, 0))
    hist_spec = pl.BlockSpec((1, HIST, wd), lambda i: (i // tiles_per_seq, 0, 0))
    vec = pl.BlockSpec((1, wd), lambda i: (0, 0))
    anyspec = pl.BlockSpec(memory_space=pl.ANY)
    return pl.pallas_call(
        functools.partial(_in_proj_conv_kernel, heads=heads, tiles_per_seq=tiles_per_seq, sub=sub, rows=rows,
                          fresh_layer=layer if fresh else None),
        grid=(m // tm,),
        in_specs=[
            pl.BlockSpec((tm, d), lambda i: (i, 0)),
            pl.BlockSpec((1, d), lambda i: (0, 0)),
            pl.BlockSpec((d, 5 * wd), lambda i: (0, 0)),
            *([] if fresh else [anyspec, anyspec]),
            hist_spec,
            pl.BlockSpec((CONV_WIDTH, SUBLANES, wd), lambda i: (0, 0, 0)),
            vec, vec, vec, vec,
        ],
        out_specs=[col, krows, krows, col, col, col, hist_spec],
        out_shape=_in_proj_outs(m, wd, stack, stack, BF16) + [jax.ShapeDtypeStruct((n_seq, HIST, wd), F32)],
        input_output_aliases={} if fresh else {3: 1, 4: 2},
        scratch_shapes=[pltpu.VMEM((HIST_PAD + tm, wd), F32), pltpu.VMEM((SUBLANES, HIST_PAD + sub, wd), F32)],
        compiler_params=_params("arbitrary"),
        name="in_proj_conv",
    )(x, g, w, *([] if fresh else [k_all, v_all]), hist, _conv_taps(dw_w), dw_b, ln_g, ln_b, g_out)


def _later_key_matrix(tk):
    j = lax.broadcasted_iota(jnp.int32, (tk, tk), 0)
    s = lax.broadcasted_iota(jnp.int32, (tk, tk), 1)
    return (j > s).astype(BF16)


def _strictly_earlier(tq, tk):
    t = lax.broadcasted_iota(jnp.int32, (tq, tk), 0)
    s = lax.broadcasted_iota(jnp.int32, (tq, tk), 1)
    return s < t


def _sb_chains(chains):
    def lane_tiled(blocks):
        return all(kb.shape[0] % SB_KEY_BLOCK == 0 for kb, _, _, _ in blocks)

    logits = []
    for q, _, blocks in chains:
        groups = [blocks] if lane_tiled(blocks) else [[blk] for blk in blocks]
        zs = []
        for group in groups:
            keys = jnp.concatenate([kb for kb, _, _, _ in group], axis=0)
            z = lax.dot_general(q, keys, (((1,), (1,)), ((), ())), preferred_element_type=F32)
            start = 0
            for kb, _, _, _ in group:
                zs.append(z[:, start:start + kb.shape[0]])
                start += kb.shape[0]
        logits.append(zs)
    staged, carries = [], []
    for (_, carry, blocks), zs in zip(chains, logits):
        row = []
        for (_, _, mask, _), z in zip(blocks, zs):
            l = jnp.log(1.0 + jnp.exp(-jnp.abs(z)))
            sp = jnp.maximum(z, 0.0) + l
            log_beta = jnp.minimum(z, 0.0) - l
            if mask is not None:
                sp = jnp.where(mask, sp, 0.0)
            hi = sp.astype(BF16)
            split = jnp.concatenate([hi, (sp - hi.astype(F32)).astype(BF16)], axis=-1)
            row.append((log_beta, split, carry))
            carry = carry + jnp.sum(sp, axis=-1, keepdims=True)
        staged.append(row)
        carries.append(carry)
    flat = [(ci, bi) for ci, (_, _, blocks) in enumerate(chains) for bi in range(len(blocks))]
    between = {}
    for later_id in dict.fromkeys(id(chains[ci][2][bi][3]) for ci, bi in flat):
        members = [(ci, bi) for ci, bi in flat if id(chains[ci][2][bi][3]) == later_id]
        later = chains[members[0][0]][2][members[0][1]][3]
        stacked = jnp.dot(jnp.concatenate([staged[ci][bi][1] for ci, bi in members], axis=0),
                          jnp.concatenate([later, later], axis=0), preferred_element_type=F32)
        start = 0
        for ci, bi in members:
            rows = staged[ci][bi][1].shape[0]
            between[ci, bi] = stacked[start:start + rows]
            start += rows
    outs = []
    for ci, (_, _, blocks) in enumerate(chains):
        ws = []
        for bi, (_, _, mask, _) in enumerate(blocks):
            log_beta, _, carry = staged[ci][bi]
            w = jnp.exp(log_beta - between[ci, bi] - carry)
            if mask is not None:
                w = jnp.where(mask, w, 0.0)
            ws.append(w.astype(BF16))
        if lane_tiled(blocks):
            out = jnp.dot(jnp.concatenate(ws, axis=1), jnp.concatenate([vb for _, vb, _, _ in blocks], axis=0),
                          preferred_element_type=F32)
        else:
            out = sum(jnp.dot(w, vb, preferred_element_type=F32) for w, (_, vb, _, _) in zip(ws, blocks))
        outs.append(out)
    return list(zip(outs, carries))


def _attn_prompt_kernel(later_ref, q_ref, k_ref, v_ref, o_ref, carry_ref, *, heads, halves, nprev):
    tb = SB_KEY_BLOCK
    i = pl.program_id(2)
    head_lanes = [slice(hh * SB_HEAD_DIM, (hh + 1) * SB_HEAD_DIM) for hh in range(heads)]
    q_rows = [slice(sub * tb, (sub + 1) * tb) for sub in range(halves)]
    diag = _strictly_earlier(tb, tb)

    def kv(blk, lanes):
        rows = pl.ds(pl.multiple_of(blk * tb, tb), tb)
        return k_ref[0, rows, lanes], v_ref[0, rows, lanes]

    def unconditional(first_step):
        later = later_ref[...]
        where, chains = [], []
        for hh, lanes in enumerate(head_lanes):
            for sub in range(halves):
                backs = range(min(sub, nprev) + 1) if first_step else range(nprev + 1)
                blocks = [kv(i * halves + sub - back, lanes) + (diag if back == 0 else None, later)
                          for back in backs]
                chains.append((q_ref[0, q_rows[sub], lanes], jnp.zeros((tb, 1), F32), blocks))
                where.append((hh, sub, lanes))
        for (hh, sub, lanes), (out, carry) in zip(where, _sb_chains(chains)):
            o_ref[0, q_rows[sub], lanes] = out
            carry_ref[hh, q_rows[sub], :] = carry

    @pl.when(i == 0)
    def _():
        unconditional(True)

    @pl.when(i > 0)
    def _():
        unconditional(False)

    def cond(state):
        n, low = state
        newest = i * halves + (halves - 1) - nprev - 1 - n
        return jnp.logical_and(newest >= 0, low < SB_DEAD_SUM)

    def body(state):
        n, _ = state
        for sub in range(halves):
            blk = i * halves + sub - nprev - 1 - n

            @pl.when(blk >= 0)
            def _():
                chains = [(q_ref[0, q_rows[sub], lanes], carry_ref[hh, q_rows[sub], :],
                           [kv(blk, lanes) + (None, later_ref[...])])
                          for hh, lanes in enumerate(head_lanes)]
                for (hh, lanes), (out, carry) in zip(enumerate(head_lanes), _sb_chains(chains)):
                    o_ref[0, q_rows[sub], lanes] += out
                    carry_ref[hh, q_rows[sub], :] = carry
        return n + 1, jnp.min(carry_ref[...])

    lax.while_loop(cond, body, (jnp.int32(0), jnp.min(carry_ref[...])))


def _attn_prompt(q, k, v, *, heads=4, halves=4, nprev=2):
    b, s, w = q.shape
    heads = min(heads, w // SB_HEAD_DIM)
    hw = heads * SB_HEAD_DIM
    tq = halves * SB_KEY_BLOCK
    assert s % tq == 0 and w % hw == 0
    kv_spec = pl.BlockSpec((1, s, hw), lambda bi, g, i: (bi, 0, g))
    q_spec = pl.BlockSpec((1, tq, hw), lambda bi, g, i: (bi, i, g))
    return pl.pallas_call(
        functools.partial(_attn_prompt_kernel, heads=heads, halves=halves, nprev=nprev),
        grid=(b, w // hw, s // tq),
        in_specs=[pl.BlockSpec((SB_KEY_BLOCK, SB_KEY_BLOCK), lambda bi, g, i: (0, 0)), q_spec, kv_spec, kv_spec],
        out_specs=q_spec,
        out_shape=jax.ShapeDtypeStruct((b, s, w), F32),
        scratch_shapes=[pltpu.VMEM((heads, tq, 1), F32)],
        compiler_params=_params("parallel", "parallel", "arbitrary"),
        name="attn_prompt",
    )(_later_key_matrix(SB_KEY_BLOCK), q, k, v)


def _attn_sample_kernel(later_ref, q_ref, kn_ref, vn_ref, kw_ref, vw_ref, kc_hbm, vc_hbm, o_ref,
                        carry_ref, kbuf, vbuf, sem, *, layer, heads, window):
    tb = SB_KEY_BLOCK
    b = pl.program_id(0)
    t = q_ref.shape[1]
    n_past = kc_hbm.shape[2] // (heads * tb)
    head_lanes = [slice(hh * SB_HEAD_DIM, (hh + 1) * SB_HEAD_DIM) for hh in range(heads)]

    def head_kv(k_rows_ref, v_rows_ref, blk, hh):
        rows = pl.ds(blk * tb * heads + hh, tb, stride=heads)
        return k_rows_ref[rows, :].astype(BF16), v_rows_ref[rows, :].astype(BF16)

    later = later_ref[...]
    chains = []
    for hh, lanes in enumerate(head_lanes):
        blocks = [(kn_ref[0, :, lanes], vn_ref[0, :, lanes], _strictly_earlier(t, t), later_ref[:t, :t])]
        blocks += [head_kv(kw_ref, vw_ref, blk, hh) + (None, later) for blk in range(window - 1, -1, -1)]
        chains.append((q_ref[0, :, lanes], jnp.zeros((t, 1), F32), blocks))
    for (hh, lanes), (out, carry) in zip(enumerate(head_lanes), _sb_chains(chains)):
        o_ref[0, :, lanes] = out
        carry_ref[hh] = carry

    def cond(state):
        j, low = state
        return jnp.logical_and(j >= 0, low < SB_DEAD_SUM)

    def body(state):
        j, _ = state
        rows = pl.ds(pl.multiple_of(j * (tb * heads), tb * heads), tb * heads)
        copies = [pltpu.make_async_copy(src.at[layer, b, rows, :], dst, sem.at[n])
                  for n, (src, dst) in enumerate(((kc_hbm, kbuf), (vc_hbm, vbuf)))]
        for cp in copies:
            cp.start()
        for cp in copies:
            cp.wait()
        chains = [(q_ref[0, :, lanes], carry_ref[hh], [head_kv(kbuf, vbuf, 0, hh) + (None, later_ref[...])])
                  for hh, lanes in enumerate(head_lanes)]
        for (hh, lanes), (out, carry) in zip(enumerate(head_lanes), _sb_chains(chains)):
            o_ref[0, :, lanes] += out
            carry_ref[hh] = carry
        return j - 1, jnp.min(carry_ref[...])

    lax.while_loop(cond, body, (jnp.int32(n_past - window - 1), jnp.min(carry_ref[...])))


def _attn_sample(q, k_new, v_new, k_cache, v_cache, layer, *, window=2):
    b, t, w = q.shape
    heads = w // SB_HEAD_DIM
    blk_rows = SB_KEY_BLOCK * heads
    n_past = k_cache.shape[2] // blk_rows
    assert k_cache.shape[2] == n_past * blk_rows and n_past % window == 0 and t <= SB_KEY_BLOCK
    new_spec = pl.BlockSpec((1, t, w), lambda bi: (bi, 0, 0))
    win_spec = pl.BlockSpec((None, None, window * blk_rows, SB_HEAD_DIM),
                            lambda bi: (layer, bi, n_past // window - 1, 0))
    anyspec = pl.BlockSpec(memory_space=pl.ANY)
    return pl.pallas_call(
        functools.partial(_attn_sample_kernel, layer=layer, heads=heads, window=window),
        grid=(b,),
        in_specs=[pl.BlockSpec((SB_KEY_BLOCK, SB_KEY_BLOCK), lambda bi: (0, 0)),
                  new_spec, new_spec, new_spec, win_spec, win_spec, anyspec, anyspec],
        out_specs=new_spec,
        out_shape=jax.ShapeDtypeStruct((b, t, w), F32),
        scratch_shapes=[pltpu.VMEM((heads, t, 1), F32),
                        pltpu.VMEM((blk_rows, SB_HEAD_DIM), F32), pltpu.VMEM((blk_rows, SB_HEAD_DIM), F32),
                        pltpu.SemaphoreType.DMA((2,))],
        compiler_params=_params("parallel"),
        name="attn_sample",
    )(_later_key_matrix(SB_KEY_BLOCK), q, k_new, v_new, k_cache, v_cache, k_cache, v_cache)


def _out_proj_kernel(x_ref, attn_ref, conv_ref, ga_ref, w_ref, o_ref):
    sb = attn_ref.shape[1]
    attn_n = _rms(attn_ref[...], ga_ref[...]).astype(BF16)
    o_ref[...] = (x_ref[...]
                  + jnp.dot(attn_n, w_ref[:sb, :], preferred_element_type=F32)
                  + jnp.dot(conv_ref[...], w_ref[sb:, :], preferred_element_type=F32))


def _out_proj(x, attn, conv_n, g_attn, w, *, tm=512):
    m, d = x.shape
    sb, c = attn.shape[1], conv_n.shape[1]
    tm = min(tm, m)
    assert m % tm == 0 and w.shape == (sb + c, d)
    return pl.pallas_call(
        _out_proj_kernel,
        grid=(m // tm,),
        in_specs=[
            pl.BlockSpec((tm, d), lambda i: (i, 0)),
            pl.BlockSpec((tm, sb), lambda i: (i, 0)),
            pl.BlockSpec((tm, c), lambda i: (i, 0)),
            pl.BlockSpec((1, sb), lambda i: (0, 0)),
            pl.BlockSpec((sb + c, d), lambda i: (0, 0)),
        ],
        out_specs=pl.BlockSpec((tm, d), lambda i: (i, 0)),
        out_shape=jax.ShapeDtypeStruct((m, d), F32),
        compiler_params=_params("parallel"),
        name="out_proj",
    )(x, attn, conv_n, g_attn, w)


def _cast_kernel(w_ref, o_ref):
    o_ref[...] = w_ref[...].astype(o_ref.dtype)


def _layer_weight_bf16(w, layer, *, block_bytes=4 * 1024 * 1024):
    _, r, c = w.shape
    tr = r
    while tr * c * 4 > block_bytes and tr % 2 == 0 and (tr // 2) % 16 == 0:
        tr //= 2
    return pl.pallas_call(
        _cast_kernel,
        grid=(r // tr,),
        in_specs=[pl.BlockSpec((None, tr, c), lambda i: (layer, i, 0))],
        out_specs=pl.BlockSpec((tr, c), lambda i: (i, 0)),
        out_shape=jax.ShapeDtypeStruct((r, c), BF16),
        compiler_params=_params("parallel"),
        name="weight_bf16",
    )(w)


def _cast_pair_kernel(a_ref, b_ref, o_ref):
    n = a_ref.shape[1]
    o_ref[:, :n] = a_ref[...].astype(o_ref.dtype)
    o_ref[:, n:] = b_ref[...].astype(o_ref.dtype)


def _gate_up_bf16(w_gate, w_up, layer, *, tf=FFN_CHUNK):
    _, d, f = w_gate.shape
    assert f % tf == 0 and w_up.shape == w_gate.shape
    spec = pl.BlockSpec((None, d, tf), lambda j: (layer, 0, j))
    return pl.pallas_call(
        _cast_pair_kernel,
        grid=(f // tf,),
        in_specs=[spec, spec],
        out_specs=pl.BlockSpec((None, d, 2 * tf), lambda j: (j, 0, 0)),
        out_shape=jax.ShapeDtypeStruct((f // tf, d, 2 * tf), BF16),
        compiler_params=_params("parallel"),
        name="gate_up_bf16",
    )(w_gate, w_up)


def _row(v):
    return v.reshape(1, -1).astype(F32)


def _layer(x, k_all, v_all, layer, depth, cache, conv_hist, lw, g_final):
    b, t, d = x.shape
    x2 = x.reshape(b * t, d)
    x2 = _ffn(x2, lw["g_ffn1"], lw["w1_gate_up"], lw["w1_down"])
    seq = lambda y: y.reshape(b, t, y.shape[1])
    conv_w = (lw["dw_w"], lw["dw_b"], lw["ln_g"], lw["ln_b"], lw["g_conv_out"])
    if cache is None:
        q, k_all, v_all, kb, vb, conv_n, new_hist = _in_proj_conv(
            x2, lw["g_mix"], lw["w_in"], k_all, v_all, layer, depth, t, conv_hist, *conv_w)
        attn = _attn_prompt(seq(q), seq(kb), seq(vb))
    else:
        q, k_all, v_all, kb, vb, a = _in_proj(x2, lw["g_mix"], lw["w_in"], k_all, v_all, layer)
        attn = _attn_sample(seq(q), seq(kb), seq(vb), cache[0], cache[1], layer)
        conv_n = _conv(seq(a), conv_hist, *conv_w).reshape(b * t, -1)
        new_hist = seq(a)[:, t - HIST:, :]
    x2 = _out_proj(x2, attn.reshape(b * t, -1), conv_n, lw["g_attn_out"], lw["w_out"])
    x2 = _ffn(x2, lw["g_ffn2"], lw["w2_gate_up"], lw["w2_down"], g_final)
    return x2.reshape(b, t, d), k_all, v_all, new_hist


def kernel(x_prompt, x_sample, cache_k, cache_v, state_conv, norm_ffn1, ffn1_gate, ffn1_up, ffn1_down, norm_mix, w_in, dw_weight, dw_bias, conv_ln_gain, conv_ln_bias, norm_attn_out, norm_conv_out, w_out, norm_ffn2, ffn2_gate, ffn2_up, ffn2_down, norm_final):
    depth = w_in.shape[0]
    conv_ch = dw_weight.shape[2]
    heads, hd = cache_k.shape[3], cache_k.shape[4]
    assert hd == SB_HEAD_DIM and x_prompt.shape[1] >= HIST and x_sample.shape[1] >= HIST
    (bp, sp, _), (bs, ts, _) = x_prompt.shape, x_sample.shape
    past = cache_k.shape[2]
    xp, xs = x_prompt, x_sample
    zero_hist = jnp.zeros((bp, HIST, conv_ch), F32)
    kp = vp = None
    ks = jnp.zeros((depth, bs * ts * heads, hd), F32)
    vs = jnp.zeros((depth, bs * ts * heads, hd), F32)
    cache = (cache_k.reshape(depth, bs, past * heads, hd), cache_v.reshape(depth, bs, past * heads, hd))
    g_final = _row(norm_final)
    conv_p, conv_s = [], []
    for l in range(depth):
        lw = dict(
            g_ffn1=_row(norm_ffn1[l]), w1_gate_up=_gate_up_bf16(ffn1_gate, ffn1_up, l),
            w1_down=_layer_weight_bf16(ffn1_down, l), g_mix=_row(norm_mix[l]), w_in=_layer_weight_bf16(w_in, l),
            dw_w=dw_weight[l].astype(F32), dw_b=_row(dw_bias[l]), ln_g=_row(conv_ln_gain[l]),
            ln_b=_row(conv_ln_bias[l]), g_attn_out=_row(norm_attn_out[l]), g_conv_out=_row(norm_conv_out[l]),
            w_out=_layer_weight_bf16(w_out, l), g_ffn2=_row(norm_ffn2[l]),
            w2_gate_up=_gate_up_bf16(ffn2_gate, ffn2_up, l), w2_down=_layer_weight_bf16(ffn2_down, l))
        closing = g_final if l == depth - 1 else None
        xp, kp, vp, cp = _layer(xp, kp, vp, l, depth, None, zero_hist, lw, closing)
        xs, ks, vs, cs = _layer(xs, ks, vs, l, depth, cache, state_conv[l], lw, closing)
        conv_p.append(cp)
        conv_s.append(cs)
    return (xp, xs,
            kp.reshape(depth, bp, sp, heads, hd), vp.reshape(depth, bp, sp, heads, hd), jnp.stack(conv_p),
            ks.reshape(depth, bs, ts, heads, hd), vs.reshape(depth, bs, ts, heads, hd), jnp.stack(conv_s))
```

```python
import functools

import jax
import jax.numpy as jnp
from jax import lax
from jax.experimental import pallas as pl
from jax.experimental.pallas import tpu as pltpu

F32 = jnp.float32
BF16 = jnp.bfloat16

EPS = 1e-6
SB_HEAD_DIM = 128
SB_KEY_BLOCK = 128
CONV_WIDTH = 31
HIST = CONV_WIDTH - 1
SUBLANES = 8
HIST_PAD = 32
ATTN_HEAD_GROUP = 2

VMEM_LIMIT = 56 * 1024 * 1024

SB_DEAD_SUM = 106.0


def _params(*sem):
    return pltpu.CompilerParams(dimension_semantics=sem, vmem_limit_bytes=VMEM_LIMIT)


def _rms(x, g):
    return x * lax.rsqrt(jnp.mean(x * x, axis=-1, keepdims=True) + EPS) * g


def _ffn_kernel(x_ref, g_ref, wg_ref, wu_ref, wd_ref, gf_ref, o_ref, h_ref, *, final_norm):
    j = pl.program_id(1)

    @pl.when(j == 0)
    def _():
        x = x_ref[...]
        h_ref[...] = _rms(x, g_ref[...]).astype(BF16)
        o_ref[...] = x

    h = h_ref[...]
    gate = jnp.dot(h, wg_ref[...], preferred_element_type=F32)
    up = jnp.dot(h, wu_ref[...], preferred_element_type=F32)
    act = (0.5 * gate * jax.nn.sigmoid(gate) * up).astype(BF16)
    o_ref[...] += jnp.dot(act, wd_ref[...], preferred_element_type=F32)

    if final_norm:
        @pl.when(j == pl.num_programs(1) - 1)
        def _():
            o_ref[...] = _rms(o_ref[...], gf_ref[...])


def _ffn(x, g, wg, wu, wd, g_final=None, *, tm=1024, tf=512):
    m, d = x.shape
    f = wg.shape[1]
    tm = min(tm, m)
    assert m % tm == 0 and f % tf == 0
    vec = pl.BlockSpec((1, d), lambda i, j: (0, 0))
    return pl.pallas_call(
        functools.partial(_ffn_kernel, final_norm=g_final is not None),
        grid=(m // tm, f // tf),
        in_specs=[
            pl.BlockSpec((tm, d), lambda i, j: (i, 0)),
            vec,
            pl.BlockSpec((d, tf), lambda i, j: (0, j)),
            pl.BlockSpec((d, tf), lambda i, j: (0, j)),
            pl.BlockSpec((tf, d), lambda i, j: (j, 0)),
            vec,
        ],
        out_specs=pl.BlockSpec((tm, d), lambda i, j: (i, 0)),
        out_shape=jax.ShapeDtypeStruct((m, d), F32),
        scratch_shapes=[pltpu.VMEM((tm, d), BF16)],
        compiler_params=_params("parallel", "arbitrary"),
        name="ffn",
    )(x, g, wg, wu, wd, g if g_final is None else g_final)


def _conv_rows(src_ref, src0, sh_ref, dw_ref, db_ref, lg_ref, lb_ref, go_ref, o_ref, out0, *, sub, rows):
    n_shift = sub + HIST_PAD - SUBLANES
    for b in range(1, SUBLANES):
        sh_ref[b, 0:n_shift, :] = src_ref[src0 + b:src0 + b + n_shift, :]
    w0 = HIST_PAD - HIST
    groups = rows // SUBLANES
    for r0 in range(0, sub, rows):
        ys = [None] * groups
        for w in range(w0, w0 + CONV_WIDTH):
            tap = dw_ref[w - w0]
            for gi in range(groups):
                base = r0 + gi * SUBLANES + (w // SUBLANES) * SUBLANES
                if w % SUBLANES == 0:
                    frames = src_ref[src0 + base:src0 + base + SUBLANES, :]
                else:
                    frames = sh_ref[w % SUBLANES, base:base + SUBLANES, :]
                term = frames * tap
                ys[gi] = term if ys[gi] is None else ys[gi] + term
        y = jnp.concatenate(ys, axis=0) + db_ref[...]
        mu = jnp.mean(y, axis=-1, keepdims=True)
        yc = y - mu
        var = jnp.mean(yc * yc, axis=-1, keepdims=True)
        y = yc * lax.rsqrt(var + EPS) * lg_ref[...] + lb_ref[...]
        y = y * jax.nn.sigmoid(y)
        o_ref[out0 + r0:out0 + r0 + rows, :] = _rms(y, go_ref[...]).astype(o_ref.dtype)


def _start_history(a_ref, hist_ref):
    a_ref[0:SUBLANES, :] = jnp.zeros((SUBLANES, a_ref.shape[1]), F32)
    a_ref[HIST_PAD - HIST:HIST_PAD, :] = hist_ref[0]


def _conv_kernel(a_ref, prev_ref, hist_ref, dw_ref, db_ref, lg_ref, lb_ref, go_ref, o_ref,
                 full_ref, sh_ref, *, tt, rows):
    i = pl.program_id(1)

    @pl.when(i == 0)
    def _():
        _start_history(full_ref, hist_ref)

    @pl.when(i > 0)
    def _():
        full_ref[0:HIST_PAD, :] = prev_ref[0]

    full_ref[HIST_PAD:HIST_PAD + tt, :] = a_ref[0]
    _conv_rows(full_ref, 0, sh_ref, dw_ref, db_ref, lg_ref, lb_ref, go_ref, o_ref.at[0], 0, sub=tt, rows=rows)


def _conv_taps(dw_w):
    return jnp.broadcast_to(dw_w[:, None, :], (CONV_WIDTH, SUBLANES, dw_w.shape[1]))


def _conv(a, hist, dw_w, dw_b, ln_g, ln_b, g_out, *, tt=256, rows=32):
    b, t, c = a.shape
    tt = min(tt, t)
    assert t % tt == 0 and tt % rows == 0 and tt % HIST_PAD == 0 and rows % SUBLANES == 0
    per = tt // HIST_PAD
    vec = pl.BlockSpec((1, c), lambda bi, i: (0, 0))
    return pl.pallas_call(
        functools.partial(_conv_kernel, tt=tt, rows=rows),
        grid=(b, t // tt),
        in_specs=[
            pl.BlockSpec((1, tt, c), lambda bi, i: (bi, i, 0)),
            pl.BlockSpec((1, HIST_PAD, c), lambda bi, i: (bi, jnp.maximum(i * per - 1, 0), 0)),
            pl.BlockSpec((1, HIST, c), lambda bi, i: (bi, 0, 0)),
            pl.BlockSpec((CONV_WIDTH, SUBLANES, c), lambda bi, i: (0, 0, 0)),
            vec, vec, vec, vec,
        ],
        out_specs=pl.BlockSpec((1, tt, c), lambda bi, i: (bi, i, 0)),
        out_shape=jax.ShapeDtypeStruct((b, t, c), BF16),
        scratch_shapes=[pltpu.VMEM((HIST_PAD + tt, c), F32), pltpu.VMEM((SUBLANES, HIST_PAD + tt, c), F32)],
        compiler_params=_params("parallel", "arbitrary"),
        name="conv",
    )(a, a, hist, _conv_taps(dw_w), dw_b, ln_g, ln_b, g_out)


def _project_qkv(h, w_ref, q_ref, k_ref, v_ref, kb_ref, vb_ref, heads):
    tm = h.shape[0]
    wd = q_ref.shape[1]

    def group(j):
        return jnp.dot(h, w_ref[:, j * wd:(j + 1) * wd], preferred_element_type=F32)

    q_ref[...] = (group(0) * (SB_HEAD_DIM ** -0.5)).astype(BF16)
    for full_ref, half_ref, j in ((k_ref, kb_ref, 1), (v_ref, vb_ref, 2)):
        u = group(j)
        half_ref[...] = u.astype(BF16)
        for hh in range(heads):
            full_ref[pl.ds(hh, tm, stride=heads), :] = u[:, hh * SB_HEAD_DIM:(hh + 1) * SB_HEAD_DIM]


def _project_glu(h, w_ref, wd):
    value = jnp.dot(h, w_ref[:, 3 * wd:4 * wd], preferred_element_type=F32)
    gate = jnp.dot(h, w_ref[:, 4 * wd:5 * wd], preferred_element_type=F32)
    return value * jax.nn.sigmoid(gate)


def _in_proj_kernel(x_ref, g_ref, w_ref, k_all_ref, v_all_ref, q_ref, k_ref, v_ref, kb_ref, vb_ref, a_ref,
                    *, heads):
    del k_all_ref, v_all_ref
    h = _rms(x_ref[...], g_ref[...]).astype(BF16)
    a_ref[...] = _project_glu(h, w_ref, q_ref.shape[1])
    _project_qkv(h, w_ref, q_ref, k_ref, v_ref, kb_ref, vb_ref, heads)


def _in_proj_conv_kernel(x_ref, g_ref, w_ref, *refs, heads, tiles_per_seq, sub, rows, fresh_layer):
    if fresh_layer is None:
        refs = refs[2:]
    (hist_ref, dw_ref, db_ref, lg_ref, lb_ref, go_ref,
     q_ref, k_ref, v_ref, kb_ref, vb_ref, c_ref, hist_out_ref, a_ref, sh_ref) = refs
    if fresh_layer is not None:
        for stack_ref in (k_ref, v_ref):
            for l in range(stack_ref.shape[0]):
                if l != fresh_layer:
                    stack_ref[l] = jnp.zeros(stack_ref.shape[1:], F32)
        k_ref, v_ref = k_ref.at[fresh_layer], v_ref.at[fresh_layer]
    tm = x_ref.shape[0]
    first = pl.program_id(0) % tiles_per_seq == 0

    @pl.when(first)
    def _():
        _start_history(a_ref, hist_ref)

    @pl.when(jnp.logical_not(first))
    def _():
        a_ref[0:HIST_PAD, :] = a_ref[tm:tm + HIST_PAD, :]

    h = _rms(x_ref[...], g_ref[...]).astype(BF16)
    a = _project_glu(h, w_ref, q_ref.shape[1])
    a_ref[HIST_PAD:HIST_PAD + tm, :] = a
    hist_out_ref[0] = a[tm - HIST:, :]
    _project_qkv(h, w_ref, q_ref, k_ref, v_ref, kb_ref, vb_ref, heads)
    for s0 in range(0, tm, sub):
        _conv_rows(a_ref, s0, sh_ref, dw_ref, db_ref, lg_ref, lb_ref, go_ref, c_ref, s0, sub=sub, rows=rows)


def _in_proj_outs(m, wd, k_all, v_all, last_dtype):
    return [
        jax.ShapeDtypeStruct((m, wd), BF16),
        jax.ShapeDtypeStruct(k_all.shape, F32),
        jax.ShapeDtypeStruct(v_all.shape, F32),
        jax.ShapeDtypeStruct((m, wd), BF16),
        jax.ShapeDtypeStruct((m, wd), BF16),
        jax.ShapeDtypeStruct((m, wd), last_dtype),
    ]


def _in_proj(x, g, w, k_all, v_all, layer, *, tm=512):
    m, d = x.shape
    wd = w.shape[1] // 5
    heads = wd // SB_HEAD_DIM
    tm = min(tm, m)
    assert m % tm == 0 and w.shape[1] == 5 * wd and k_all.shape[1:] == (m * heads, SB_HEAD_DIM)
    col = pl.BlockSpec((tm, wd), lambda i: (i, 0))
    rows = pl.BlockSpec((None, tm * heads, SB_HEAD_DIM), lambda i: (layer, i, 0))
    anyspec = pl.BlockSpec(memory_space=pl.ANY)
    return pl.pallas_call(
        functools.partial(_in_proj_kernel, heads=heads),
        grid=(m // tm,),
        in_specs=[
            pl.BlockSpec((tm, d), lambda i: (i, 0)),
            pl.BlockSpec((1, d), lambda i: (0, 0)),
            pl.BlockSpec((d, 5 * wd), lambda i: (0, 0)),
            anyspec, anyspec,
        ],
        out_specs=[col, rows, rows, col, col, col],
        out_shape=_in_proj_outs(m, wd, k_all, v_all, F32),
        input_output_aliases={3: 1, 4: 2},
        compiler_params=_params("parallel"),
        name="in_proj",
    )(x, g, w, k_all, v_all)


def _in_proj_conv(x, g, w, k_all, v_all, layer, depth, seq_len, hist, dw_w, dw_b, ln_g, ln_b, g_out,
                  *, tm=256, sub=128, rows=32):
    m, d = x.shape
    wd = w.shape[1] // 5
    heads = wd // SB_HEAD_DIM
    n_seq = m // seq_len
    assert seq_len % tm == 0 and tm % sub == 0 and sub % rows == 0 and rows % SUBLANES == 0 and tm >= HIST_PAD
    fresh = k_all is None
    stack = jax.ShapeDtypeStruct((depth, m * heads, SB_HEAD_DIM), F32)
    assert w.shape[1] == 5 * wd and hist.shape == (n_seq, HIST, wd)
    assert fresh or (k_all.shape == stack.shape and v_all.shape == stack.shape)
    tiles_per_seq = seq_len // tm
    col = pl.BlockSpec((tm, wd), lambda i: (i, 0))
    if fresh:
        krows = pl.BlockSpec((depth, tm * heads, SB_HEAD_DIM), lambda i: (0, i, 0))
    else:
        krows = pl.BlockSpec((None, tm * heads, SB_HEAD_DIM), lambda i: (layer, i, 0))
    hist_spec = pl.BlockSpec((1, HIST, wd), lambda i: (i // tiles_per_seq, 0, 0))
    vec = pl.BlockSpec((1, wd), lambda i: (0, 0))
    anyspec = pl.BlockSpec(memory_space=pl.ANY)
    return pl.pallas_call(
        functools.partial(_in_proj_conv_kernel, heads=heads, tiles_per_seq=tiles_per_seq, sub=sub, rows=rows,
                          fresh_layer=layer if fresh else None),
        grid=(m // tm,),
        in_specs=[
            pl.BlockSpec((tm, d), lambda i: (i, 0)),
            pl.BlockSpec((1, d), lambda i: (0, 0)),
            pl.BlockSpec((d, 5 * wd), lambda i: (0, 0)),
            *([] if fresh else [anyspec, anyspec]),
            hist_spec,
            pl.BlockSpec((CONV_WIDTH, SUBLANES, wd), lambda i: (0, 0, 0)),
            vec, vec, vec, vec,
        ],
        out_specs=[col, krows, krows, col, col, col, hist_spec],
        out_shape=_in_proj_outs(m, wd, stack, stack, BF16) + [jax.ShapeDtypeStruct((n_seq, HIST, wd), F32)],
        input_output_aliases={} if fresh else {3: 1, 4: 2},
        scratch_shapes=[pltpu.VMEM((HIST_PAD + tm, wd), F32), pltpu.VMEM((SUBLANES, HIST_PAD + sub, wd), F32)],
        compiler_params=_params("arbitrary"),
        name="in_proj_conv",
    )(x, g, w, *([] if fresh else [k_all, v_all]), hist, _conv_taps(dw_w), dw_b, ln_g, ln_b, g_out)


def _later_key_matrix(tk):
    j = lax.broadcasted_iota(jnp.int32, (tk, tk), 0)
    s = lax.broadcasted_iota(jnp.int32, (tk, tk), 1)
    return (j > s).astype(BF16)


def _strictly_earlier(tq, tk):
    t = lax.broadcasted_iota(jnp.int32, (tq, tk), 0)
    s = lax.broadcasted_iota(jnp.int32, (tq, tk), 1)
    return s < t


def _sb_chains(chains):
    def lane_tiled(blocks):
        return all(kb.shape[0] % SB_KEY_BLOCK == 0 for kb, _, _, _ in blocks)

    logits = []
    for q, _, blocks in chains:
        groups = [blocks] if lane_tiled(blocks) else [[blk] for blk in blocks]
        zs = []
        for group in groups:
            keys = jnp.concatenate([kb for kb, _, _, _ in group], axis=0)
            z = lax.dot_general(q, keys, (((1,), (1,)), ((), ())), preferred_element_type=F32)
            start = 0
            for kb, _, _, _ in group:
                zs.append(z[:, start:start + kb.shape[0]])
                start += kb.shape[0]
        logits.append(zs)
    staged, carries = [], []
    for (_, carry, blocks), zs in zip(chains, logits):
        row = []
        for (_, _, mask, _), z in zip(blocks, zs):
            l = jnp.log(1.0 + jnp.exp(-jnp.abs(z)))
            sp = jnp.maximum(z, 0.0) + l
            log_beta = z - sp
            if mask is not None:
                sp = jnp.where(mask, sp, 0.0)
            hi = sp.astype(BF16)
            split = jnp.concatenate([hi, (sp - hi.astype(F32)).astype(BF16)], axis=-1)
            row.append((log_beta, split, carry))
            carry = carry + jnp.sum(sp, axis=-1, keepdims=True)
        staged.append(row)
        carries.append(carry)
    flat = [(ci, bi) for ci, (_, _, blocks) in enumerate(chains) for bi in range(len(blocks))]
    between = {}
    for later_id in dict.fromkeys(id(chains[ci][2][bi][3]) for ci, bi in flat):
        members = [(ci, bi) for ci, bi in flat if id(chains[ci][2][bi][3]) == later_id]
        later = chains[members[0][0]][2][members[0][1]][3]
        stacked = jnp.dot(jnp.concatenate([staged[ci][bi][1] for ci, bi in members], axis=0),
                          jnp.concatenate([later, later], axis=0), preferred_element_type=F32)
        start = 0
        for ci, bi in members:
            rows = staged[ci][bi][1].shape[0]
            between[ci, bi] = stacked[start:start + rows]
            start += rows
    outs = []
    for ci, (_, _, blocks) in enumerate(chains):
        ws = []
        for bi, (_, _, mask, _) in enumerate(blocks):
            log_beta, _, carry = staged[ci][bi]
            w = jnp.exp(log_beta - between[ci, bi] - carry)
            if mask is not None:
                w = jnp.where(mask, w, 0.0)
            ws.append(w.astype(BF16))
        if lane_tiled(blocks):
            out = jnp.dot(jnp.concatenate(ws, axis=1), jnp.concatenate([vb for _, vb, _, _ in blocks], axis=0),
                          preferred_element_type=F32)
        else:
            out = sum(jnp.dot(w, vb, preferred_element_type=F32) for w, (_, vb, _, _) in zip(ws, blocks))
        outs.append(out)
    return list(zip(outs, carries))


def _attn_prompt_kernel(later_ref, q_ref, k_ref, v_ref, o_ref, carry_ref, *, heads, halves, nprev):
    tb = SB_KEY_BLOCK
    i = pl.program_id(2)
    head_lanes = [slice(hh * SB_HEAD_DIM, (hh + 1) * SB_HEAD_DIM) for hh in range(heads)]
    q_rows = [slice(sub * tb, (sub + 1) * tb) for sub in range(halves)]
    diag = _strictly_earlier(tb, tb)

    def kv(blk, lanes):
        rows = pl.ds(pl.multiple_of(blk * tb, tb), tb)
        return k_ref[0, rows, lanes], v_ref[0, rows, lanes]

    def unconditional(first_step):
        later = later_ref[...]
        for h0 in range(0, heads, ATTN_HEAD_GROUP):
            where, chains = [], []
            for hh in range(h0, min(h0 + ATTN_HEAD_GROUP, heads)):
                lanes = head_lanes[hh]
                for sub in range(halves):
                    backs = range(min(sub, nprev) + 1) if first_step else range(nprev + 1)
                    blocks = [kv(i * halves + sub - back, lanes) + (diag if back == 0 else None, later)
                              for back in backs]
                    chains.append((q_ref[0, q_rows[sub], lanes], jnp.zeros((tb, 1), F32), blocks))
                    where.append((hh, sub, lanes))
            for (hh, sub, lanes), (out, carry) in zip(where, _sb_chains(chains)):
                o_ref[0, q_rows[sub], lanes] = out
                carry_ref[hh, q_rows[sub], :] = carry

    @pl.when(i == 0)
    def _():
        unconditional(True)

    @pl.when(i > 0)
    def _():
        unconditional(False)

    def cond(state):
        n, low = state
        newest = i * halves + (halves - 1) - nprev - 1 - n
        return jnp.logical_and(newest >= 0, low < SB_DEAD_SUM)

    def body(state):
        n, _ = state
        for sub in range(halves):
            blk = i * halves + sub - nprev - 1 - n

            @pl.when(blk >= 0)
            def _():
                chains = [(q_ref[0, q_rows[sub], lanes], carry_ref[hh, q_rows[sub], :],
                           [kv(blk, lanes) + (None, later_ref[...])])
                          for hh, lanes in enumerate(head_lanes)]
                for (hh, lanes), (out, carry) in zip(enumerate(head_lanes), _sb_chains(chains)):
                    o_ref[0, q_rows[sub], lanes] += out
                    carry_ref[hh, q_rows[sub], :] = carry
        return n + 1, jnp.min(carry_ref[...])

    lax.while_loop(cond, body, (jnp.int32(0), jnp.min(carry_ref[...])))


def _attn_prompt(q, k, v, *, heads=4, halves=4, nprev=2):
    b, s, w = q.shape
    heads = min(heads, w // SB_HEAD_DIM)
    hw = heads * SB_HEAD_DIM
    tq = halves * SB_KEY_BLOCK
    assert s % tq == 0 and w % hw == 0
    kv_spec = pl.BlockSpec((1, s, hw), lambda bi, g, i: (bi, 0, g))
    q_spec = pl.BlockSpec((1, tq, hw), lambda bi, g, i: (bi, i, g))
    return pl.pallas_call(
        functools.partial(_attn_prompt_kernel, heads=heads, halves=halves, nprev=nprev),
        grid=(b, w // hw, s // tq),
        in_specs=[pl.BlockSpec((SB_KEY_BLOCK, SB_KEY_BLOCK), lambda bi, g, i: (0, 0)), q_spec, kv_spec, kv_spec],
        out_specs=q_spec,
        out_shape=jax.ShapeDtypeStruct((b, s, w), F32),
        scratch_shapes=[pltpu.VMEM((heads, tq, 1), F32)],
        compiler_params=_params("parallel", "parallel", "arbitrary"),
        name="attn_prompt",
    )(_later_key_matrix(SB_KEY_BLOCK), q, k, v)


def _attn_sample_kernel(later_ref, q_ref, kn_ref, vn_ref, kw_ref, vw_ref, kc_hbm, vc_hbm, o_ref,
                        carry_ref, kbuf, vbuf, sem, *, layer, heads, window):
    tb = SB_KEY_BLOCK
    b = pl.program_id(0)
    t = q_ref.shape[1]
    n_past = kc_hbm.shape[2] // (heads * tb)
    head_lanes = [slice(hh * SB_HEAD_DIM, (hh + 1) * SB_HEAD_DIM) for hh in range(heads)]

    def head_kv(k_rows_ref, v_rows_ref, blk, hh):
        rows = pl.ds(blk * tb * heads + hh, tb, stride=heads)
        return k_rows_ref[rows, :].astype(BF16), v_rows_ref[rows, :].astype(BF16)

    later = later_ref[...]
    chains = []
    for hh, lanes in enumerate(head_lanes):
        blocks = [(kn_ref[0, :, lanes], vn_ref[0, :, lanes], _strictly_earlier(t, t), later_ref[:t, :t])]
        blocks += [head_kv(kw_ref, vw_ref, blk, hh) + (None, later) for blk in range(window - 1, -1, -1)]
        chains.append((q_ref[0, :, lanes], jnp.zeros((t, 1), F32), blocks))
    for (hh, lanes), (out, carry) in zip(enumerate(head_lanes), _sb_chains(chains)):
        o_ref[0, :, lanes] = out
        carry_ref[hh] = carry

    def cond(state):
        j, low = state
        return jnp.logical_and(j >= 0, low < SB_DEAD_SUM)

    def body(state):
        j, _ = state
        rows = pl.ds(pl.multiple_of(j * (tb * heads), tb * heads), tb * heads)
        copies = [pltpu.make_async_copy(src.at[layer, b, rows, :], dst, sem.at[n])
                  for n, (src, dst) in enumerate(((kc_hbm, kbuf), (vc_hbm, vbuf)))]
        for cp in copies:
            cp.start()
        for cp in copies:
            cp.wait()
        chains = [(q_ref[0, :, lanes], carry_ref[hh], [head_kv(kbuf, vbuf, 0, hh) + (None, later_ref[...])])
                  for hh, lanes in enumerate(head_lanes)]
        for (hh, lanes), (out, carry) in zip(enumerate(head_lanes), _sb_chains(chains)):
            o_ref[0, :, lanes] += out
            carry_ref[hh] = carry
        return j - 1, jnp.min(carry_ref[...])

    lax.while_loop(cond, body, (jnp.int32(n_past - window - 1), jnp.min(carry_ref[...])))


def _attn_sample(q, k_new, v_new, k_cache, v_cache, layer, *, window=2):
    b, t, w = q.shape
    heads = w // SB_HEAD_DIM
    blk_rows = SB_KEY_BLOCK * heads
    n_past = k_cache.shape[2] // blk_rows
    assert k_cache.shape[2] == n_past * blk_rows and n_past % window == 0 and t <= SB_KEY_BLOCK
    new_spec = pl.BlockSpec((1, t, w), lambda bi: (bi, 0, 0))
    win_spec = pl.BlockSpec((None, None, window * blk_rows, SB_HEAD_DIM),
                            lambda bi: (layer, bi, n_past // window - 1, 0))
    anyspec = pl.BlockSpec(memory_space=pl.ANY)
    return pl.pallas_call(
        functools.partial(_attn_sample_kernel, layer=layer, heads=heads, window=window),
        grid=(b,),
        in_specs=[pl.BlockSpec((SB_KEY_BLOCK, SB_KEY_BLOCK), lambda bi: (0, 0)),
                  new_spec, new_spec, new_spec, win_spec, win_spec, anyspec, anyspec],
        out_specs=new_spec,
        out_shape=jax.ShapeDtypeStruct((b, t, w), F32),
        scratch_shapes=[pltpu.VMEM((heads, t, 1), F32),
                        pltpu.VMEM((blk_rows, SB_HEAD_DIM), F32), pltpu.VMEM((blk_rows, SB_HEAD_DIM), F32),
                        pltpu.SemaphoreType.DMA((2,))],
        compiler_params=_params("parallel"),
        name="attn_sample",
    )(_later_key_matrix(SB_KEY_BLOCK), q, k_new, v_new, k_cache, v_cache, k_cache, v_cache)


def _out_proj_kernel(x_ref, attn_ref, conv_ref, ga_ref, w_ref, o_ref):
    sb = attn_ref.shape[1]
    attn_n = _rms(attn_ref[...], ga_ref[...]).astype(BF16)
    o_ref[...] = (x_ref[...]
                  + jnp.dot(attn_n, w_ref[:sb, :], preferred_element_type=F32)
                  + jnp.dot(conv_ref[...], w_ref[sb:, :], preferred_element_type=F32))


def _out_proj(x, attn, conv_n, g_attn, w, *, tm=512):
    m, d = x.shape
    sb, c = attn.shape[1], conv_n.shape[1]
    tm = min(tm, m)
    assert m % tm == 0 and w.shape == (sb + c, d)
    return pl.pallas_call(
        _out_proj_kernel,
        grid=(m // tm,),
        in_specs=[
            pl.BlockSpec((tm, d), lambda i: (i, 0)),
            pl.BlockSpec((tm, sb), lambda i: (i, 0)),
            pl.BlockSpec((tm, c), lambda i: (i, 0)),
            pl.BlockSpec((1, sb), lambda i: (0, 0)),
            pl.BlockSpec((sb + c, d), lambda i: (0, 0)),
        ],
        out_specs=pl.BlockSpec((tm, d), lambda i: (i, 0)),
        out_shape=jax.ShapeDtypeStruct((m, d), F32),
        compiler_params=_params("parallel"),
        name="out_proj",
    )(x, attn, conv_n, g_attn, w)


def _cast_kernel(w_ref, o_ref):
    o_ref[...] = w_ref[...].astype(o_ref.dtype)


def _layer_weight_bf16(w, layer, *, block_bytes=4 * 1024 * 1024):
    _, r, c = w.shape
    tr = r
    while tr * c * 4 > block_bytes and tr % 2 == 0 and (tr // 2) % 16 == 0:
        tr //= 2
    return pl.pallas_call(
        _cast_kernel,
        grid=(r // tr,),
        in_specs=[pl.BlockSpec((None, tr, c), lambda i: (layer, i, 0))],
        out_specs=pl.BlockSpec((tr, c), lambda i: (i, 0)),
        out_shape=jax.ShapeDtypeStruct((r, c), BF16),
        compiler_params=_params("parallel"),
        name="weight_bf16",
    )(w)


def _row(v):
    return v.reshape(1, -1).astype(F32)


def _layer(x, k_all, v_all, layer, depth, cache, conv_hist, lw, g_final):
    b, t, d = x.shape
    x2 = x.reshape(b * t, d)
    x2 = _ffn(x2, lw["g_ffn1"], lw["w1_gate"], lw["w1_up"], lw["w1_down"])
    seq = lambda y: y.reshape(b, t, y.shape[1])
    conv_w = (lw["dw_w"], lw["dw_b"], lw["ln_g"], lw["ln_b"], lw["g_conv_out"])
    if cache is None:
        q, k_all, v_all, kb, vb, conv_n, new_hist = _in_proj_conv(
            x2, lw["g_mix"], lw["w_in"], k_all, v_all, layer, depth, t, conv_hist, *conv_w)
        attn = _attn_prompt(seq(q), seq(kb), seq(vb))
    else:
        q, k_all, v_all, kb, vb, a = _in_proj(x2, lw["g_mix"], lw["w_in"], k_all, v_all, layer)
        attn = _attn_sample(seq(q), seq(kb), seq(vb), cache[0], cache[1], layer)
        conv_n = _conv(seq(a), conv_hist, *conv_w).reshape(b * t, -1)
        new_hist = seq(a)[:, t - HIST:, :]
    x2 = _out_proj(x2, attn.reshape(b * t, -1), conv_n, lw["g_attn_out"], lw["w_out"])
    x2 = _ffn(x2, lw["g_ffn2"], lw["w2_gate"], lw["w2_up"], lw["w2_down"], g_final)
    return x2.reshape(b, t, d), k_all, v_all, new_hist


def kernel(x_prompt, x_sample, cache_k, cache_v, state_conv, norm_ffn1, ffn1_gate, ffn1_up, ffn1_down, norm_mix, w_in, dw_weight, dw_bias, conv_ln_gain, conv_ln_bias, norm_attn_out, norm_conv_out, w_out, norm_ffn2, ffn2_gate, ffn2_up, ffn2_down, norm_final):
    depth = w_in.shape[0]
    conv_ch = dw_weight.shape[2]
    heads, hd = cache_k.shape[3], cache_k.shape[4]
    assert hd == SB_HEAD_DIM and x_prompt.shape[1] >= HIST and x_sample.shape[1] >= HIST
    (bp, sp, _), (bs, ts, _) = x_prompt.shape, x_sample.shape
    past = cache_k.shape[2]
    xp, xs = x_prompt, x_sample
    zero_hist = jnp.zeros((bp, HIST, conv_ch), F32)
    kp = vp = None
    ks = jnp.zeros((depth, bs * ts * heads, hd), F32)
    vs = jnp.zeros((depth, bs * ts * heads, hd), F32)
    cache = (cache_k.reshape(depth, bs, past * heads, hd), cache_v.reshape(depth, bs, past * heads, hd))
    g_final = _row(norm_final)
    conv_p, conv_s = [], []
    for l in range(depth):
        lw = dict(
            g_ffn1=_row(norm_ffn1[l]), w1_gate=_layer_weight_bf16(ffn1_gate, l), w1_up=_layer_weight_bf16(ffn1_up, l),
            w1_down=_layer_weight_bf16(ffn1_down, l), g_mix=_row(norm_mix[l]), w_in=_layer_weight_bf16(w_in, l),
            dw_w=dw_weight[l].astype(F32), dw_b=_row(dw_bias[l]), ln_g=_row(conv_ln_gain[l]),
            ln_b=_row(conv_ln_bias[l]), g_attn_out=_row(norm_attn_out[l]), g_conv_out=_row(norm_conv_out[l]),
            w_out=_layer_weight_bf16(w_out, l), g_ffn2=_row(norm_ffn2[l]), w2_gate=_layer_weight_bf16(ffn2_gate, l),
            w2_up=_layer_weight_bf16(ffn2_up, l), w2_down=_layer_weight_bf16(ffn2_down, l))
        closing = g_final if l == depth - 1 else None
        xp, kp, vp, cp = _layer(xp, kp, vp, l, depth, None, zero_hist, lw, closing)
        xs, ks, vs, cs = _layer(xs, ks, vs, l, depth, cache, state_conv[l], lw, closing)
        conv_p.append(cp)
        conv_s.append(cs)
    return (xp, xs,
            kp.reshape(depth, bp, sp, heads, hd), vp.reshape(depth, bp, sp, heads, hd), jnp.stack(conv_p),
            ks.reshape(depth, bs, ts, heads, hd), vs.reshape(depth, bs, ts, heads, hd), jnp.stack(conv_s))
```
